```python
import math
import jax, jax.numpy as jnp
from jax import lax
import numpy as np

D_MODEL = 1024
BATCH = 32
SEQ = 2048
DEPTH = 1
DEC_BATCH = 4
DEC_SEQ = 8192
PAST_LEN = 128

N_META = 16
RWKV_HEADS = 8
RWKV_HEAD_DIM = 64
RWKV_WIDTH = RWKV_HEADS * RWKV_HEAD_DIM
DIFF_HEADS = 4
DIFF_QK_DIM = 64
DIFF_V_DIM = 2 * DIFF_QK_DIM
DIFF_WIDTH = DIFF_HEADS * DIFF_V_DIM
MIX_WIDTH = RWKV_WIDTH + DIFF_WIDTH
DIFF_QK_WIDTH = DIFF_HEADS * 2 * DIFF_QK_DIM
IN_WIDTH = 3 * RWKV_WIDTH + 2 * DIFF_QK_WIDTH + DIFF_WIDTH
DECAY_LORA = 64
AAA_LORA = 64
GATE_LORA = 160
CONV_WIDTH = 3
ROPE_THETA = 500000.0
ROPE_DIMS = DIFF_QK_DIM // 4
D_FF = -(-8 * D_MODEL // (3 * 256)) * 256
Q_BLOCK = 128
NORM_EPS = 1e-6
GN_EPS = 64e-5
DECAY_SCALE = math.exp(-0.5)

kernel_name = "hymba_rwkv7_diffattn_encoder"


def rms_norm(x, g):
    xf = x.astype(jnp.float32)
    y = xf * lax.rsqrt(jnp.mean(xf * xf, axis=-1, keepdims=True) + NORM_EPS)
    return (y * g.astype(jnp.float32)).astype(x.dtype)


def shift_prev(x):
    return jnp.pad(x, ((0, 0), (1, 0), (0, 0)))[:, :-1]


def shift_next(x):
    return jnp.pad(x, ((0, 0), (0, 1), (0, 0)))[:, 1:]


def centred_conv(x, w):
    return w[0] * shift_prev(x) + w[1] * x + w[2] * shift_next(x)


def wkv7_scan(r, w, k, v, a_, b_, reverse):
    B, T, H, N = r.shape

    def step(S, inp):
        rt, wt, kt, vt, at, bt = inp
        sa = jnp.einsum('bhij,bhj->bhi', S, at)
        S = S * wt[:, :, None, :] + sa[..., None] * bt[:, :, None, :] + vt[..., None] * kt[:, :, None, :]
        return S, jnp.einsum('bhij,bhj->bhi', S, rt)

    xs = tuple(jnp.moveaxis(t, 1, 0) for t in (r, w, k, v, a_, b_))
    S0 = jnp.zeros((B, H, N, N), jnp.float32)
    _, y = lax.scan(step, S0, xs, reverse=reverse)
    return jnp.moveaxis(y, 0, 1)


def rwkv7_mixer(h, r, k, v, p):
    B, T, _ = h.shape
    H, N = RWKV_HEADS, RWKV_HEAD_DIM
    hf = h.astype(jnp.float32)
    r, k, v = (t.astype(jnp.float32) for t in (r, k, v))
    delta = 0.5 * (shift_prev(hf) + shift_next(hf)) - hf
    xg = hf + delta * p['mix_g']
    g = jax.nn.sigmoid(xg @ p['gate_g1']) @ p['gate_g2']
    kk = (k * p['k_k']).reshape(B, T, H, N)
    kk = kk / jnp.maximum(jnp.linalg.norm(kk, axis=-1, keepdims=True), 1e-12)
    rh = r.reshape(B, T, H, N)
    vh = v.reshape(B, T, H, N)
    y_sum = jnp.zeros((B, T, H, N), jnp.float32)
    bonus = jnp.zeros((B, T, H, N), jnp.float32)
    for d in range(2):
        xw = hf + delta * p['mix_w'][d]
        xa = hf + delta * p['mix_a'][d]
        w = jnp.exp(-DECAY_SCALE * jax.nn.sigmoid(p['decay_w0'][d] + jnp.tanh(xw @ p['decay_w1'][d]) @ p['decay_w2'][d]))
        a = jax.nn.sigmoid(p['aaa_a0'][d] + (xa @ p['aaa_a1'][d]) @ p['aaa_a2'][d])
        kd = k * (1.0 + (a - 1.0) * p['k_a'])
        kdh = kd.reshape(B, T, H, N)
        ah = a.reshape(B, T, H, N)
        y_sum = y_sum + wkv7_scan(rh, w.reshape(B, T, H, N), kdh, vh, -kk, kk * ah, reverse=(d == 1))
        bonus = bonus + jnp.sum(rh * kdh * p['r_k'], axis=-1, keepdims=True) * vh
    mu = jnp.mean(y_sum, axis=-1, keepdims=True)
    var = jnp.mean(jnp.square(y_sum - mu), axis=-1, keepdims=True)
    yn = (y_sum - mu) * lax.rsqrt(var + GN_EPS)
    yn = yn * p['lnx_g'].reshape(H, N) + p['lnx_b'].reshape(H, N)
    out = (yn + bonus).reshape(B, T, RWKV_WIDTH) * g
    return out.astype(h.dtype)


def partial_rope(x):
    T = x.shape[1]
    half = ROPE_DIMS // 2
    freqs = ROPE_THETA ** (-jnp.arange(0, ROPE_DIMS, 2, dtype=jnp.float32) / ROPE_DIMS)
    ang = jnp.arange(T, dtype=jnp.float32)[:, None] * freqs[None, :]
    cos = jnp.cos(ang)[None, :, None, None, :]
    sin = jnp.sin(ang)[None, :, None, None, :]
    xf = x.astype(jnp.float32)
    x1, x2, rest = xf[..., :half], xf[..., half:ROPE_DIMS], xf[..., ROPE_DIMS:]
    out = jnp.concatenate([x1 * cos - x2 * sin, x2 * cos + x1 * sin, rest], axis=-1)
    return out.astype(x.dtype)


def diff_attention(q, k, v, p, lambda_init):
    B, T, _ = q.shape
    H, d = DIFF_HEADS, DIFF_QK_DIM
    q = q.reshape(B, T, H, 2, d)
    k = k.reshape(B, T, H, 2, d)
    v = v.reshape(B, T, H, DIFF_V_DIM).astype(jnp.float32)
    q = partial_rope(rms_norm(q, p['q_norm_g'])).astype(jnp.float32) * (d ** -0.5)
    k = partial_rope(rms_norm(k, p['k_norm_g'])).astype(jnp.float32)
    lam_v = p['diff_lambdas'].astype(jnp.float32)
    lam = jnp.exp(jnp.sum(lam_v[0] * lam_v[1])) - jnp.exp(jnp.sum(lam_v[2] * lam_v[3])) + lambda_init

    def attend(qb):
        s = jnp.einsum('bqhcd,bkhcd->bhcqk', qb, k)
        pr = jax.nn.softmax(s, axis=-1)
        amap = pr[:, :, 0] - lam * pr[:, :, 1]
        return jnp.einsum('bhqk,bkhe->bqhe', amap, v)

    meta_out = attend(q[:, :N_META])
    n_blk = (T - N_META) // Q_BLOCK
    qb = q[:, N_META:].reshape(B, n_blk, Q_BLOCK, H, 2, d).transpose(1, 0, 2, 3, 4, 5)
    ob = lax.map(attend, qb)
    real_out = ob.transpose(1, 0, 2, 3, 4).reshape(B, T - N_META, H, DIFF_V_DIM)
    o = jnp.concatenate([meta_out, real_out], axis=1)
    o = rms_norm(o, p['subln_g']) * (1.0 - lambda_init)
    return o.reshape(B, T, DIFF_WIDTH).astype(q.dtype)


def encoder_layer(x, p, lambda_init):
    n = rms_norm(x, p['ln1_g'])
    proj = n @ p['w_in']
    c0 = 3 * RWKV_WIDTH
    rkv = centred_conv(proj[..., :c0], p['conv_rkv'])
    r, k, v = rkv[..., :RWKV_WIDTH], rkv[..., RWKV_WIDTH:2 * RWKV_WIDTH], rkv[..., 2 * RWKV_WIDTH:]
    qd = proj[..., c0:c0 + DIFF_QK_WIDTH]
    kd = proj[..., c0 + DIFF_QK_WIDTH:c0 + 2 * DIFF_QK_WIDTH]
    vd = proj[..., c0 + 2 * DIFF_QK_WIDTH:]
    ya = rwkv7_mixer(n, r, k, v, p)
    yb = diff_attention(qd, kd, vd, p, lambda_init)
    x = x + jnp.concatenate([ya, yb], axis=-1) @ p['w_out']
    n2 = rms_norm(x, p['ln2_g'])
    x = x + (jax.nn.silu(n2 @ p['w_gate']) * (n2 @ p['w_up'])) @ p['w_down']
    return x


def trunk(x, meta_tokens, p):
    B = x.shape[0]
    meta = jnp.broadcast_to(meta_tokens.astype(x.dtype)[None], (B, N_META, D_MODEL))
    h = jnp.concatenate([meta, x], axis=1)
    for l in range(DEPTH):
        pl = {name: arr[l] for name, arr in p.items()}
        lambda_init = 0.8 - 0.6 * math.exp(-0.3 * l)
        h = encoder_layer(h, pl, lambda_init)
    return h[:, N_META:]


def setup_inputs(seed: int = 0) -> dict:
    key = jax.random.key(seed)
    ks = iter(jax.random.split(key, 40))

    def nrm(shape, scale):
        return jax.random.normal(next(ks), shape, jnp.float32) * scale

    L = DEPTH
    conv_c = 1.0 + nrm((L, 1, 3 * RWKV_WIDTH), 0.05)
    conv_s = nrm((L, 2, 3 * RWKV_WIDTH), 0.1)
    conv_rkv = jnp.concatenate([conv_s[:, :1], conv_c, conv_s[:, 1:]], axis=1)
    return {
        'x_prompt': nrm((BATCH, SEQ, D_MODEL), 1.0),
        'x_sample': nrm((DEC_BATCH, DEC_SEQ, D_MODEL), 1.0),
        'meta_tokens': nrm((N_META, D_MODEL), 1.0),
        'ln1_g': 1.0 + nrm((L, D_MODEL), 0.05),
        'w_in': nrm((L, D_MODEL, IN_WIDTH), D_MODEL ** -0.5),
        'conv_rkv': conv_rkv,
        'mix_w': jax.random.uniform(next(ks), (L, 2, D_MODEL), jnp.float32),
        'mix_a': jax.random.uniform(next(ks), (L, 2, D_MODEL), jnp.float32),
        'mix_g': jax.random.uniform(next(ks), (L, D_MODEL), jnp.float32),
        'decay_w0': -1.0 + nrm((L, 2, RWKV_WIDTH), 0.5),
        'decay_w1': nrm((L, 2, D_MODEL, DECAY_LORA), D_MODEL ** -0.5),
        'decay_w2': nrm((L, 2, DECAY_LORA, RWKV_WIDTH), 0.1 * DECAY_LORA ** -0.5),
        'aaa_a0': nrm((L, 2, RWKV_WIDTH), 0.1),
        'aaa_a1': nrm((L, 2, D_MODEL, AAA_LORA), D_MODEL ** -0.5),
        'aaa_a2': nrm((L, 2, AAA_LORA, RWKV_WIDTH), 0.1 * AAA_LORA ** -0.5),
        'gate_g1': nrm((L, D_MODEL, GATE_LORA), D_MODEL ** -0.5),
        'gate_g2': nrm((L, GATE_LORA, RWKV_WIDTH), GATE_LORA ** -0.5),
        'k_k': 0.85 + nrm((L, RWKV_WIDTH), 0.05),
        'k_a': 1.0 + nrm((L, RWKV_WIDTH), 0.05),
        'r_k': nrm((L, RWKV_HEADS, RWKV_HEAD_DIM), 0.1),
        'lnx_g': 1.0 + nrm((L, RWKV_WIDTH), 0.05),
        'lnx_b': nrm((L, RWKV_WIDTH), 0.01),
        'q_norm_g': 1.0 + nrm((L, DIFF_QK_DIM), 0.05),
        'k_norm_g': 1.0 + nrm((L, DIFF_QK_DIM), 0.05),
        'diff_lambdas': nrm((L, 4, DIFF_QK_DIM), 0.1),
        'subln_g': 1.0 + nrm((L, DIFF_V_DIM), 0.05),
        'w_out': nrm((L, MIX_WIDTH, D_MODEL), MIX_WIDTH ** -0.5),
        'ln2_g': 1.0 + nrm((L, D_MODEL), 0.05),
        'w_gate': nrm((L, D_MODEL, D_FF), D_MODEL ** -0.5),
        'w_up': nrm((L, D_MODEL, D_FF), D_MODEL ** -0.5),
        'w_down': nrm((L, D_FF, D_MODEL), D_FF ** -0.5),
    }


def reference(x_prompt, x_sample, meta_tokens, ln1_g, w_in, conv_rkv, mix_w, mix_a, mix_g,
              decay_w0, decay_w1, decay_w2, aaa_a0, aaa_a1, aaa_a2, gate_g1, gate_g2,
              k_k, k_a, r_k, lnx_g, lnx_b, q_norm_g, k_norm_g, diff_lambdas, subln_g,
              w_out, ln2_g, w_gate, w_up, w_down):
    p = {
        'ln1_g': ln1_g, 'w_in': w_in, 'conv_rkv': conv_rkv,
        'mix_w': mix_w, 'mix_a': mix_a, 'mix_g': mix_g,
        'decay_w0': decay_w0, 'decay_w1': decay_w1, 'decay_w2': decay_w2,
        'aaa_a0': aaa_a0, 'aaa_a1': aaa_a1, 'aaa_a2': aaa_a2,
        'gate_g1': gate_g1, 'gate_g2': gate_g2,
        'k_k': k_k, 'k_a': k_a, 'r_k': r_k, 'lnx_g': lnx_g, 'lnx_b': lnx_b,
        'q_norm_g': q_norm_g, 'k_norm_g': k_norm_g, 'diff_lambdas': diff_lambdas,
        'subln_g': subln_g, 'w_out': w_out, 'ln2_g': ln2_g,
        'w_gate': w_gate, 'w_up': w_up, 'w_down': w_down,
    }
    y_prompt = trunk(x_prompt, meta_tokens, p)
    y_sample = trunk(x_sample, meta_tokens, p)
    return (y_prompt, y_sample)
```

```python
import functools
import math

import jax
import jax.numpy as jnp
from jax import lax
from jax.experimental import pallas as pl
from jax.experimental.pallas import tpu as pltpu

F32 = jnp.float32
BF16 = jnp.bfloat16

D_MODEL = 1024
N_META = 16
RWKV_HEADS = 8
HEAD_DIM = 64
RWKV_WIDTH = 512
DIFF_HEADS = 4
DIFF_QK_DIM = 64
DIFF_V_DIM = 128
DIFF_WIDTH = 512
IN_WIDTH = 3072
DECAY_LORA = 64
AAA_LORA = 64
GATE_LORA = 160
GATE_LORA_PAD = 256
LORA_BLOCK = 512
CAT_WIDTH = IN_WIDTH + 2 * LORA_BLOCK
ROPE_THETA = 500000.0
ROPE_DIMS = 16
ROPE_HALF = 8
D_FF = 2816
NORM_EPS = 1e-6
GN_EPS = 64e-5
DECAY_SCALE = math.exp(-0.5)
LAMBDA_INIT = 0.8 - 0.6 * math.exp(0.0)

LANES = 128
SUBLANES = 8
HALO = SUBLANES
CHUNK = 64
VMEM_LIMIT = 56 * 1024 * 1024


def _const_spec(shape):
    zeros = (0,) * len(shape)
    return pl.BlockSpec(shape, lambda *_: zeros)


def _seg_sum(x, seg_ref):
    seg = seg_ref[...]
    parts = []
    for c in range(x.shape[1] // 256):
        xs = x[:, 256 * c:256 * (c + 1)].astype(BF16)
        parts.append(jnp.dot(xs, seg, preferred_element_type=F32))
    return jnp.concatenate(parts, axis=1)


def _roll_lanes(x, shift):
    parts = [pltpu.roll(x[:, LANES * c:LANES * (c + 1)], shift, 1) for c in range(x.shape[1] // LANES)]
    return jnp.concatenate(parts, axis=1)


def _pre_compute(xx, tt, tabs, w, outs):
    (cos_ref, sa_ref, sb_ref) = tabs
    (ln1_ref, wcat_ref, conv_ref, w0_ref, w2_ref, a0_ref, a2_ref, g2_ref, kk_ref, ka_ref, rk_ref,
     qg_ref, kg_ref, seg_ref) = w
    (o_r, o_v, o_kk, o_lw, o_kd, o_b, o_g, o_bonus, o_q, o_k, o_vd) = outs
    rows = tt + 2 * HALO

    ms = jnp.mean(xx * xx, axis=-1, keepdims=True)
    n = xx * lax.rsqrt(ms + NORM_EPS) * ln1_ref[...]
    p = jnp.dot(n.astype(BF16), wcat_ref[...], preferred_element_type=F32)

    def prev_rows(a):
        return pltpu.roll(a, 1, 0)[HALO:HALO + tt]

    def next_rows(a):
        return pltpu.roll(a, rows - 1, 0)[HALO:HALO + tt]

    c0 = 3 * RWKV_WIDTH
    rkv_in = p[:, :c0]
    conv = conv_ref[...]
    rkv = conv[0:1] * prev_rows(rkv_in) + conv[1:2] * rkv_in[HALO:HALO + tt] + conv[2:3] * next_rows(rkv_in)
    r = rkv[:, :RWKV_WIDTH]
    k = rkv[:, RWKV_WIDTH:2 * RWKV_WIDTH]
    v = rkv[:, 2 * RWKV_WIDTH:]

    l_self = p[HALO:HALO + tt, IN_WIDTH:IN_WIDTH + LORA_BLOCK]
    l_nbr = p[:, IN_WIDTH + LORA_BLOCK:]
    lora = l_self + prev_rows(l_nbr) + next_rows(l_nbr)

    tw = jnp.tanh(lora[:, 0:128]).astype(BF16)
    zw = jnp.dot(tw, w2_ref[...], preferred_element_type=F32) + w0_ref[...]
    lw = -DECAY_SCALE * jax.nn.sigmoid(zw)
    za = jnp.dot(lora[:, 128:256].astype(BF16), a2_ref[...], preferred_element_type=F32) + a0_ref[...]
    aa = jax.nn.sigmoid(za)
    sg = jax.nn.sigmoid(lora[:, 256:512]).astype(BF16)
    g = jnp.dot(sg, g2_ref[...], preferred_element_type=F32)

    kk = k * kk_ref[...]
    kk = kk * lax.rsqrt(jnp.maximum(_seg_sum(kk * kk, seg_ref), 1e-24))
    k_a = ka_ref[...]
    kd0 = k * (1.0 + (aa[:, :RWKV_WIDTH] - 1.0) * k_a)
    kd1 = k * (1.0 + (aa[:, RWKV_WIDTH:] - 1.0) * k_a)
    bonus = _seg_sum(r * rk_ref[...] * (kd0 + kd1), seg_ref) * v

    o_r[...] = r
    o_v[...] = v
    o_kk[...] = kk
    o_lw[0] = lw[:, :RWKV_WIDTH]
    o_lw[1] = lw[:, RWKV_WIDTH:]
    o_kd[0] = kd0
    o_kd[1] = kd1
    o_b[0] = kk * aa[:, :RWKV_WIDTH]
    o_b[1] = kk * aa[:, RWKV_WIDTH:]
    o_g[...] = g
    o_bonus[...] = bonus

    cos_t = jnp.concatenate([cos_ref[...]] * 4, axis=1)
    sa_t = jnp.concatenate([sa_ref[...]] * 4, axis=1)
    sb_t = jnp.concatenate([sb_ref[...]] * 4, axis=1)

    def qk_norm_rope(xq, g_ref):
        ssq = _seg_sum(xq * xq, seg_ref) * (1.0 / DIFF_QK_DIM)
        xn = xq * lax.rsqrt(ssq + NORM_EPS) * g_ref[...]
        return xn * cos_t + _roll_lanes(xn, LANES - ROPE_HALF) * sa_t + _roll_lanes(xn, ROPE_HALF) * sb_t

    qd = p[HALO:HALO + tt, c0:c0 + DIFF_WIDTH]
    kd = p[HALO:HALO + tt, c0 + DIFF_WIDTH:c0 + 2 * DIFF_WIDTH]
    vd = p[HALO:HALO + tt, c0 + 2 * DIFF_WIDTH:c0 + 3 * DIFF_WIDTH]
    o_q[...] = (qk_norm_rope(qd, qg_ref) * (DIFF_QK_DIM ** -0.5)).astype(BF16)
    o_k[...] = qk_norm_rope(kd, kg_ref).astype(BF16)
    o_vd[...] = vd.astype(BF16)


def _pre_real_kernel(nt, tt, xm_ref, xp_ref, xn_ref, meta_ref, cos_ref, sa_ref, sb_ref, *rest):
    w, outs = rest[:14], rest[14:]
    j = pl.program_id(0) % nt
    xp = jnp.where(j == 0, meta_ref[N_META - HALO:N_META, :], xp_ref[...])
    xn = jnp.where(j == nt - 1, 0.0, xn_ref[...])
    xx = jnp.concatenate([xp, xm_ref[...], xn], axis=0)
    _pre_compute(xx, tt, (cos_ref, sa_ref, sb_ref), w, outs)


def _pre_meta_kernel(xn_ref, meta_ref, cos_ref, sa_ref, sb_ref, *rest):
    w, outs = rest[:14], rest[14:]
    xx = jnp.concatenate([jnp.zeros((HALO, D_MODEL), F32), meta_ref[...], xn_ref[0]], axis=0)
    _pre_compute(xx, N_META, (cos_ref, sa_ref, sb_ref), w, outs)


def _pre_out_shapes(rows):
    one = jax.ShapeDtypeStruct((rows, RWKV_WIDTH), F32)
    two = jax.ShapeDtypeStruct((2, rows, RWKV_WIDTH), F32)
    half = jax.ShapeDtypeStruct((rows, DIFF_WIDTH), BF16)
    return (one, one, one, two, two, two, one, one, half, half, half)


def _pre_out_specs(tt):
    one = pl.BlockSpec((tt, RWKV_WIDTH), lambda i: (i, 0))
    two = pl.BlockSpec((2, tt, RWKV_WIDTH), lambda i: (0, i, 0))
    return (one, one, one, two, two, two, one, one, one, one, one)


def _pre_weight_specs():
    shapes = [(1, D_MODEL), (D_MODEL, CAT_WIDTH), (3, 3 * RWKV_WIDTH), (1, 2 * RWKV_WIDTH),
              (2 * DECAY_LORA, 2 * RWKV_WIDTH), (1, 2 * RWKV_WIDTH), (2 * AAA_LORA, 2 * RWKV_WIDTH),
              (GATE_LORA_PAD, RWKV_WIDTH), (1, RWKV_WIDTH), (1, RWKV_WIDTH), (1, RWKV_WIDTH),
              (1, DIFF_WIDTH), (1, DIFF_WIDTH), (256, 256)]
    return [_const_spec(s) for s in shapes]


def _pre_real(x2, meta, tabs_real, weights, seq, tt):
    rows = x2.shape[0]
    nt = seq // tt
    hb = tt // HALO
    last_halo = rows // HALO - 1
    in_specs = [
        pl.BlockSpec((tt, D_MODEL), lambda i: (i, 0)),
        pl.BlockSpec((HALO, D_MODEL), lambda i: (jnp.maximum(i * hb - 1, 0), 0)),
        pl.BlockSpec((HALO, D_MODEL), lambda i: (jnp.minimum((i + 1) * hb, last_halo), 0)),
        _const_spec((N_META, D_MODEL)),
    ] + [pl.BlockSpec((tt, LANES), lambda i: (i % nt, 0))] * 3 + _pre_weight_specs()
    return pl.pallas_call(
        functools.partial(_pre_real_kernel, nt, tt),
        grid=(rows // tt,),
        in_specs=in_specs,
        out_specs=_pre_out_specs(tt),
        out_shape=_pre_out_shapes(rows),
        compiler_params=pltpu.CompilerParams(dimension_semantics=("parallel",), vmem_limit_bytes=VMEM_LIMIT),
        name="pre_real",
    )(x2, x2, x2, meta, *tabs_real, *weights)


def _pre_meta(x3, meta, tabs_meta, weights):
    batch = x3.shape[0]
    in_specs = [
        pl.BlockSpec((1, HALO, D_MODEL), lambda b: (b, 0, 0)),
        _const_spec((N_META, D_MODEL)),
    ] + [_const_spec((N_META, LANES))] * 3 + _pre_weight_specs()
    return pl.pallas_call(
        _pre_meta_kernel,
        grid=(batch,),
        in_specs=in_specs,
        out_specs=_pre_out_specs(N_META),
        out_shape=_pre_out_shapes(batch * N_META),
        compiler_params=pltpu.CompilerParams(dimension_semantics=("parallel",), vmem_limit_bytes=VMEM_LIMIT),
        name="pre_meta",
    )(x3, meta, *tabs_meta, *weights)


def _scan_kernel(ncr, r_ref, v_ref, kk_ref, lw_ref, kd_ref, b_ref,
                 mr_ref, mv_ref, mkk_ref, mlw_ref, mkd_ref, mb_ref, y_ref, s_ref):
    d = pl.program_id(1)
    c = pl.program_id(2)
    fwd = d == 0
    is_meta = jnp.logical_and(fwd, c == 0)

    @pl.when(c == 0)
    def _():
        s_ref[...] = jnp.zeros_like(s_ref)

    @pl.when(jnp.logical_or(fwd, c < ncr))
    def _():
        pad = jnp.zeros((CHUNK - N_META, RWKV_WIDTH), F32)

        def pick(real_ref, meta_ref, stacked):
            real = real_ref[0, 0] if stacked else real_ref[0]
            meta = meta_ref[0, 0] if stacked else meta_ref[0]
            return jnp.where(is_meta, jnp.concatenate([meta, pad], axis=0), real)

        r = pick(r_ref, mr_ref, False)
        v = pick(v_ref, mv_ref, False)
        kk = pick(kk_ref, mkk_ref, False)
        lw = pick(lw_ref, mlw_ref, True)
        kd = pick(kd_ref, mkd_ref, True)
        b = pick(b_ref, mb_ref, True)

        row = lax.broadcasted_iota(jnp.int32, (CHUNK, CHUNK), 0)
        colc = lax.broadcasted_iota(jnp.int32, (CHUNK, CHUNK), 1)
        sgn = jnp.where(fwd, 1, -1)
        m_incl = ((row - colc) * sgn >= 0).astype(BF16)
        lw_hi = lw.astype(BF16)
        lw_lo = (lw - lw_hi.astype(F32)).astype(BF16)
        ci = (jnp.dot(m_incl, lw_hi, preferred_element_type=F32)
              + jnp.dot(m_incl, lw_lo, preferred_element_type=F32))
        tot = jnp.sum(lw, axis=0, keepdims=True)
        e_incl = jnp.exp(ci)
        e_inv = jnp.exp(-ci)
        e_rest = jnp.exp(tot - ci)
        gam = jnp.exp(tot)
        rt_all = r * e_incl
        at_all = -kk * jnp.exp(ci - lw)
        kt_all = kd * e_inv
        bt_all = b * e_inv
        kh_all = kd * e_rest
        bh_all = b * e_rest

        rowt = lax.broadcasted_iota(jnp.int32, (CHUNK, LANES), 0)
        lane = lax.broadcasted_iota(jnp.int32, (CHUNK, LANES), 1)
        src = lane % CHUNK
        strict = (rowt - src) * sgn > 0
        incl = (rowt - src) * sgn >= 0
        left = lane < HEAD_DIM

        for pr in range(RWKV_HEADS // 2):
            sl = slice(LANES * pr, LANES * (pr + 1))
            at, rt, bt, kt = at_all[:, sl], rt_all[:, sl], bt_all[:, sl], kt_all[:, sl]
            bh, kh = bh_all[:, sl], kh_all[:, sl]
            v_sw = pltpu.roll(v[:, sl], HEAD_DIM, 1)
            gam_p = gam[:, sl]
            rhs_g = jnp.concatenate([bt, kt], axis=0).astype(BF16)
            rhs_v = jnp.concatenate([v_sw, v_sw], axis=0).astype(BF16)
            r_acc = jnp.zeros((CHUNK, LANES), F32)
            y_acc = jnp.zeros((CHUNK, LANES), F32)
            m_rows = []
            n_rows = []
            for hh in range(2):
                nat = left if hh == 0 else jnp.logical_not(left)
                oth = jnp.logical_not(nat)
                at_n = jnp.where(nat, at, 0.0)
                rt_n = jnp.where(nat, rt, 0.0)
                lhs_g = jnp.concatenate([at_n, rt_n], axis=0).astype(BF16)
                gmat = lax.dot_general(lhs_g, rhs_g, (((1,), (1,)), ((), ())), preferred_element_type=F32)
                g_top, g_bot = gmat[:CHUNK], gmat[CHUNK:]
                pmat = jnp.where(jnp.logical_and(strict, left), g_top, 0.0)
                a_ak = jnp.where(jnp.logical_and(strict, jnp.logical_not(left)), g_top, 0.0)
                akv = jnp.dot(a_ak.astype(BF16), rhs_v, preferred_element_type=F32)
                z = at_n + jnp.where(oth, akv, 0.0)
                for _ in range(5):
                    rhs = jnp.concatenate([z, pmat], axis=1).astype(BF16)
                    res = jnp.dot(pmat[:, :CHUNK].astype(BF16), rhs, preferred_element_type=F32)
                    z = z + res[:, :LANES]
                    pmat = res[:, LANES:]
                z = z + jnp.dot(pmat[:, :CHUNK].astype(BF16), z.astype(BF16), preferred_element_type=F32)
                rhs2 = jnp.concatenate([z, jnp.where(oth, v_sw, 0.0)], axis=0).astype(BF16)
                out_top = jnp.dot(jnp.where(incl, g_bot, 0.0).astype(BF16), rhs2, preferred_element_type=F32)
                bk = jnp.concatenate([jnp.where(nat, bh, 0.0), jnp.where(nat, kh, 0.0)], axis=0)
                bk_t = bk.T[HEAD_DIM * hh:HEAD_DIM * (hh + 1)]
                out_bot = jnp.dot(bk_t.astype(BF16), rhs2, preferred_element_type=F32)
                r_acc = r_acc + jnp.where(nat, out_top + rt, 0.0)
                y_acc = y_acc + jnp.where(oth, out_top, 0.0)
                diag = jnp.where(lane == rowt + HEAD_DIM * hh, gam_p, 0.0)
                m_rows.append(jnp.where(nat, out_bot, 0.0) + diag)
                n_rows.append(jnp.where(oth, out_bot, 0.0))
            m_pair = jnp.concatenate(m_rows, axis=0).astype(BF16)
            n_pair = jnp.concatenate(n_rows, axis=0)
            s_b = s_ref[pr].astype(BF16)
            y_sw = jnp.dot(r_acc.astype(BF16), s_b, preferred_element_type=F32) + y_acc
            s_ref[pr] = jnp.dot(m_pair, s_b, preferred_element_type=F32) + n_pair
            y_ref[0, 0, :, sl] = pltpu.roll(y_sw, HEAD_DIM, 1)


def _scan(real, meta, batch, seq):
    ncr = seq // CHUNK

    def ridx(d, c):
        return jnp.where(d == 0, jnp.maximum(c - 1, 0), jnp.maximum(ncr - 1 - c, 0))

    one = pl.BlockSpec((1, CHUNK, RWKV_WIDTH), lambda b, d, c: (b, ridx(d, c), 0))
    two = pl.BlockSpec((1, 1, CHUNK, RWKV_WIDTH), lambda b, d, c: (d, b, ridx(d, c), 0))
    mone = pl.BlockSpec((1, N_META, RWKV_WIDTH), lambda b, d, c: (b, 0, 0))
    mtwo = pl.BlockSpec((1, 1, N_META, RWKV_WIDTH), lambda b, d, c: (d, b, 0, 0))
    return pl.pallas_call(
        functools.partial(_scan_kernel, ncr),
        grid=(batch, 2, ncr + 1),
        in_specs=[one, one, one, two, two, two, mone, mone, mone, mtwo, mtwo, mtwo],
        out_specs=pl.BlockSpec((1, 1, CHUNK, RWKV_WIDTH), lambda b, d, c: (d, b, ridx(d, c), 0)),
        out_shape=jax.ShapeDtypeStruct((2, batch, seq, RWKV_WIDTH), F32),
        scratch_shapes=[pltpu.VMEM((RWKV_HEADS // 2, LANES, LANES), F32)],
        compiler_params=pltpu.CompilerParams(
            dimension_semantics=("parallel", "parallel", "arbitrary"), vmem_limit_bytes=VMEM_LIMIT),
        name="wkv_scan",
    )(*real, *meta)


def _attn_kernel(nkb, tk, q_ref, k_ref, v_ref, km_ref, vm_ref, lam_ref, sg_ref, o_ref):
    q = q_ref[0]
    tq = q.shape[0]
    lane = lax.broadcasted_iota(jnp.int32, (tq, LANES), 1)
    zero = jnp.zeros_like(q)
    qc = (jnp.where(lane < DIFF_QK_DIM, q, zero), jnp.where(lane >= DIFF_QK_DIM, q, zero))
    contract = (((1,), (1,)), ((), ()))

    def block(kb, vb, state):
        new = []
        for comp in range(2):
            m, l, acc = state[comp]
            s = lax.dot_general(qc[comp], kb, contract, preferred_element_type=F32)
            m_new = jnp.maximum(m, jnp.max(s, axis=-1, keepdims=True))
            alpha = jnp.exp(m - m_new)
            pexp = jnp.exp(s - m_new)
            l_new = alpha * l + jnp.sum(pexp, axis=-1, keepdims=True)
            acc_new = alpha * acc + jnp.dot(pexp.astype(BF16), vb, preferred_element_type=F32)
            new.append((m_new, l_new, acc_new))
        return tuple(new)

    init = tuple((jnp.full((tq, 1), -jnp.inf, F32), jnp.zeros((tq, 1), F32), jnp.zeros((tq, LANES), F32))
                 for _ in range(2))
    state = block(km_ref[0], vm_ref[0], init)

    def body(i, st):
        start = pl.multiple_of(i * tk, tk)
        return block(k_ref[0, pl.ds(start, tk), :], v_ref[0, pl.ds(start, tk), :], st)

    state = lax.fori_loop(0, nkb, body, state)
    lam_v = lam_ref[...]
    lam = (jnp.exp(jnp.sum(lam_v[0:1] * lam_v[1:2], axis=-1, keepdims=True))
           - jnp.exp(jnp.sum(lam_v[2:3] * lam_v[3:4], axis=-1, keepdims=True)) + LAMBDA_INIT)
    (_, l0, a0), (_, l1, a1) = state
    o = a0 / l0 - lam * (a1 / l1)
    o = o * lax.rsqrt(jnp.mean(o * o, axis=-1, keepdims=True) + NORM_EPS) * sg_ref[...]
    o_ref[0] = o * (1.0 - LAMBDA_INIT)


def _attention(q, k, v, km, vm, lam, subln, tq, tk):
    batch, seq, _ = q.shape
    return pl.pallas_call(
        functools.partial(_attn_kernel, seq // tk, tk),
        grid=(batch, DIFF_HEADS, seq // tq),
        in_specs=[
            pl.BlockSpec((1, tq, LANES), lambda b, h, i: (b, i, h)),
            pl.BlockSpec((1, seq, LANES), lambda b, h, i: (b, 0, h)),
            pl.BlockSpec((1, seq, LANES), lambda b, h, i: (b, 0, h)),
            pl.BlockSpec((1, N_META, LANES), lambda b, h, i: (b, 0, h)),
            pl.BlockSpec((1, N_META, LANES), lambda b, h, i: (b, 0, h)),
            _const_spec((4, DIFF_QK_DIM)),
            _const_spec((1, DIFF_V_DIM)),
        ],
        out_specs=pl.BlockSpec((1, tq, LANES), lambda b, h, i: (b, i, h)),
        out_shape=jax.ShapeDtypeStruct((batch, seq, DIFF_WIDTH), F32),
        compiler_params=pltpu.CompilerParams(
            dimension_semantics=("parallel", "parallel", "parallel"), vmem_limit_bytes=VMEM_LIMIT),
        name="diff_attn",
    )(q, k, v, km, vm, lam, subln)


def _post_kernel(x_ref, y_ref, bonus_ref, g_ref, yb_ref, lng_ref, lnb_ref, seg_ref, wout_ref, ln2_ref,
                 wg_ref, wu_ref, wd_ref, o_ref):
    ys = y_ref[0] + y_ref[1]
    inv_n = 1.0 / HEAD_DIM
    mu = _seg_sum(ys, seg_ref) * inv_n
    dev = ys - mu
    var = _seg_sum(dev * dev, seg_ref) * inv_n
    yn = dev * lax.rsqrt(var + GN_EPS) * lng_ref[...] + lnb_ref[...]
    ya = (yn + bonus_ref[...]) * g_ref[...]
    mix = jnp.concatenate([ya, yb_ref[...]], axis=1).astype(BF16)
    x1 = x_ref[...] + jnp.dot(mix, wout_ref[...], preferred_element_type=F32)
    n2 = x1 * lax.rsqrt(jnp.mean(x1 * x1, axis=-1, keepdims=True) + NORM_EPS) * ln2_ref[...]
    n2 = n2.astype(BF16)
    gate = jnp.dot(n2, wg_ref[...], preferred_element_type=F32)
    up = jnp.dot(n2, wu_ref[...], preferred_element_type=F32)
    hid = (gate * jax.nn.sigmoid(gate) * up).astype(BF16)
    o_ref[...] = x1 + jnp.dot(hid, wd_ref[...], preferred_element_type=F32)


def _post(x2, y, bonus, g, yb, weights, tt):
    rows = x2.shape[0]
    row512 = pl.BlockSpec((tt, RWKV_WIDTH), lambda i: (i, 0))
    in_specs = [
        pl.BlockSpec((tt, D_MODEL), lambda i: (i, 0)),
        pl.BlockSpec((2, tt, RWKV_WIDTH), lambda i: (0, i, 0)),
        row512, row512, row512,
        _const_spec((1, RWKV_WIDTH)), _const_spec((1, RWKV_WIDTH)), _const_spec((256, 256)),
        _const_spec((D_MODEL, D_MODEL)), _const_spec((1, D_MODEL)),
        _const_spec((D_MODEL, D_FF)), _const_spec((D_MODEL, D_FF)), _const_spec((D_FF, D_MODEL)),
    ]
    return pl.pallas_call(
        _post_kernel,
        grid=(rows // tt,),
        in_specs=in_specs,
        out_specs=pl.BlockSpec((tt, D_MODEL), lambda i: (i, 0)),
        out_shape=jax.ShapeDtypeStruct((rows, D_MODEL), F32),
        compiler_params=pltpu.CompilerParams(dimension_semantics=("parallel",), vmem_limit_bytes=VMEM_LIMIT),
        name="post",
    )(x2, y, bonus, g, yb, *weights)


def _rope_tables(total):
    freqs = ROPE_THETA ** (-jnp.arange(0, ROPE_DIMS, 2, dtype=F32) / ROPE_DIMS)
    ang = jnp.arange(total, dtype=F32)[:, None] * freqs[None, :]
    cos, sin = jnp.cos(ang), jnp.sin(ang)
    ones = jnp.ones((total, DIFF_QK_DIM - ROPE_DIMS), F32)
    zeros = jnp.zeros((total, ROPE_HALF), F32)
    zrest = jnp.zeros((total, DIFF_QK_DIM - ROPE_DIMS), F32)
    cos64 = jnp.concatenate([cos, cos, ones], axis=1)
    sa64 = jnp.concatenate([-sin, zeros, zrest], axis=1)
    sb64 = jnp.concatenate([zeros, sin, zrest], axis=1)
    return tuple(jnp.concatenate([t, t], axis=1) for t in (cos64, sa64, sb64))


def _prepare_weights(p):
    def lora_cols(scale_fn):
        cols = [scale_fn(p['mix_w'][0])[:, None] * p['decay_w1'][0],
                scale_fn(p['mix_w'][1])[:, None] * p['decay_w1'][1],
                scale_fn(p['mix_a'][0])[:, None] * p['aaa_a1'][0],
                scale_fn(p['mix_a'][1])[:, None] * p['aaa_a1'][1],
                scale_fn(p['mix_g'])[:, None] * p['gate_g1'],
                jnp.zeros((D_MODEL, GATE_LORA_PAD - GATE_LORA), F32)]
        return jnp.concatenate(cols, axis=1)

    wcat = jnp.concatenate([p['w_in'], lora_cols(lambda m: 1.0 - m), lora_cols(lambda m: 0.5 * m)], axis=1)

    def blockdiag(w):
        z = jnp.zeros_like(w[0])
        return jnp.concatenate([jnp.concatenate([w[0], z], axis=1), jnp.concatenate([z, w[1]], axis=1)], axis=0)

    g2 = jnp.concatenate([p['gate_g2'], jnp.zeros((GATE_LORA_PAD - GATE_LORA, RWKV_WIDTH), F32)], axis=0)
    seg = (jnp.arange(256)[:, None] // HEAD_DIM == jnp.arange(256)[None, :] // HEAD_DIM).astype(BF16)
    pre_w = (
        p['ln1_g'][None, :], wcat.astype(BF16), p['conv_rkv'],
        p['decay_w0'].reshape(1, 2 * RWKV_WIDTH), blockdiag(p['decay_w2']).astype(BF16),
        p['aaa_a0'].reshape(1, 2 * RWKV_WIDTH), blockdiag(p['aaa_a2']).astype(BF16),
        g2.astype(BF16), p['k_k'][None, :], p['k_a'][None, :], p['r_k'].reshape(1, RWKV_WIDTH),
        jnp.tile(p['q_norm_g'], DIFF_WIDTH // DIFF_QK_DIM)[None, :],
        jnp.tile(p['k_norm_g'], DIFF_WIDTH // DIFF_QK_DIM)[None, :], seg,
    )
    post_w = (
        p['lnx_g'][None, :], p['lnx_b'][None, :], seg, p['w_out'].astype(BF16), p['ln2_g'][None, :],
        p['w_gate'].astype(BF16), p['w_up'].astype(BF16), p['w_down'].astype(BF16),
    )
    return pre_w, post_w


def _tile_rows(seq, cap):
    tt = min(seq, cap)
    assert seq % tt == 0 and tt % CHUNK == 0
    return tt


def _trunk(x, meta, pre_w, post_w, lam, subln, tt_pre, tt_post, tq, tk):
    batch, seq, _ = x.shape
    rows = batch * seq
    x2 = x.reshape(rows, D_MODEL)
    tabs = _rope_tables(N_META + seq)
    tabs_meta = tuple(t[:N_META] for t in tabs)
    tabs_real = tuple(t[N_META:] for t in tabs)

    real = _pre_real(x2, meta, tabs_real, pre_w, seq, _tile_rows(seq, tt_pre))
    metao = _pre_meta(x, meta, tabs_meta, pre_w)
    (r, v, kk, lw, kd, b, g, bonus, q, k, vd) = real
    (mr, mv, mkk, mlw, mkd, mb, _, _, _, mk, mvd) = metao

    def r3(a, n):
        return a.reshape(a.shape[:-2] + (batch, n, a.shape[-1]))

    y = _scan((r3(r, seq), r3(v, seq), r3(kk, seq), r3(lw, seq), r3(kd, seq), r3(b, seq)),
              (r3(mr, N_META), r3(mv, N_META), r3(mkk, N_META), r3(mlw, N_META), r3(mkd, N_META),
               r3(mb, N_META)), batch, seq)
    yb = _attention(r3(q, seq), r3(k, seq), r3(vd, seq), r3(mk, N_META), r3(mvd, N_META), lam, subln,
                    _tile_rows(seq, tq), _tile_rows(seq, tk))
    out = _post(x2, y.reshape(2, rows, RWKV_WIDTH), bonus, g, yb.reshape(rows, DIFF_WIDTH), post_w,
                _tile_rows(seq, tt_post))
    return out.reshape(batch, seq, D_MODEL)


def kernel(x_prompt, x_sample, meta_tokens, ln1_g, w_in, conv_rkv, mix_w, mix_a, mix_g, decay_w0, decay_w1,
           decay_w2, aaa_a0, aaa_a1, aaa_a2, gate_g1, gate_g2, k_k, k_a, r_k, lnx_g, lnx_b, q_norm_g, k_norm_g,
           diff_lambdas, subln_g, w_out, ln2_g, w_gate, w_up, w_down):
    params = dict(ln1_g=ln1_g, w_in=w_in, conv_rkv=conv_rkv, mix_w=mix_w, mix_a=mix_a, mix_g=mix_g,
                  decay_w0=decay_w0, decay_w1=decay_w1, decay_w2=decay_w2, aaa_a0=aaa_a0, aaa_a1=aaa_a1,
                  aaa_a2=aaa_a2, gate_g1=gate_g1, gate_g2=gate_g2, k_k=k_k, k_a=k_a, r_k=r_k, lnx_g=lnx_g,
                  lnx_b=lnx_b, q_norm_g=q_norm_g, k_norm_g=k_norm_g, w_out=w_out, ln2_g=ln2_g, w_gate=w_gate,
                  w_up=w_up, w_down=w_down)
    p = {name: arr[0] for name, arr in params.items()}
    pre_w, post_w = _prepare_weights(p)
    lam = diff_lambdas[0]
    subln = subln_g[0][None, :]
    outs = []
    for x in (x_prompt, x_sample):
        outs.append(_trunk(x, meta_tokens, pre_w, post_w, lam, subln, 256, 256, 512, 512))
    return tuple(outs)
```

```python
import functools
import math

import jax
import jax.numpy as jnp
from jax import lax
from jax.experimental import pallas as pl
from jax.experimental.pallas import tpu as pltpu

F32 = jnp.float32
BF16 = jnp.bfloat16

D_MODEL = 1024
N_META = 16
RWKV_HEADS = 8
HEAD_DIM = 64
RWKV_WIDTH = 512
DIFF_HEADS = 4
DIFF_QK_DIM = 64
DIFF_V_DIM = 128
DIFF_WIDTH = 512
IN_WIDTH = 3072
DECAY_LORA = 64
AAA_LORA = 64
GATE_LORA = 160
GATE_LORA_PAD = 256
LORA_BLOCK = 512
CAT_WIDTH = IN_WIDTH + 2 * LORA_BLOCK
ROPE_THETA = 500000.0
ROPE_DIMS = 16
ROPE_HALF = 8
D_FF = 2816
NORM_EPS = 1e-6
GN_EPS = 64e-5
DECAY_SCALE = math.exp(-0.5)
LAMBDA_INIT = 0.8 - 0.6 * math.exp(0.0)

LANES = 128
SUBLANES = 8
HALO = SUBLANES
CHUNK = 64
ATTN_GROUP = 4
VMEM_LIMIT = 56 * 1024 * 1024


def _const_spec(shape):
    zeros = (0,) * len(shape)
    return pl.BlockSpec(shape, lambda *_: zeros)


def _seg_sum(x, seg_ref):
    seg = seg_ref[...]
    parts = []
    for c in range(x.shape[1] // 256):
        xs = x[:, 256 * c:256 * (c + 1)].astype(BF16)
        parts.append(jnp.dot(xs, seg, preferred_element_type=F32))
    return jnp.concatenate(parts, axis=1)


def _roll_lanes(x, shift):
    parts = [pltpu.roll(x[:, LANES * c:LANES * (c + 1)], shift, 1) for c in range(x.shape[1] // LANES)]
    return jnp.concatenate(parts, axis=1)


def _pre_compute(xx, tt, tabs, w, outs):
    (cos_ref, sa_ref, sb_ref) = tabs
    (ln1_ref, wcat_ref, conv_ref, w0_ref, w2_ref, a0_ref, a2_ref, g2_ref, kk_ref, ka_ref, rk_ref,
     qg_ref, kg_ref, seg_ref) = w
    (o_r, o_v, o_kk, o_lw, o_kd, o_b, o_g, o_bonus, o_q, o_k, o_vd) = outs
    rows = tt + 2 * HALO

    ms = jnp.mean(xx * xx, axis=-1, keepdims=True)
    n = xx * lax.rsqrt(ms + NORM_EPS) * ln1_ref[...]
    p = jnp.dot(n.astype(BF16), wcat_ref[...], preferred_element_type=F32)

    def prev_rows(a):
        return pltpu.roll(a, 1, 0)[HALO:HALO + tt]

    def next_rows(a):
        return pltpu.roll(a, rows - 1, 0)[HALO:HALO + tt]

    c0 = 3 * RWKV_WIDTH
    rkv_in = p[:, :c0]
    conv = conv_ref[...]
    rkv = conv[0:1] * prev_rows(rkv_in) + conv[1:2] * rkv_in[HALO:HALO + tt] + conv[2:3] * next_rows(rkv_in)
    r = rkv[:, :RWKV_WIDTH]
    k = rkv[:, RWKV_WIDTH:2 * RWKV_WIDTH]
    v = rkv[:, 2 * RWKV_WIDTH:]

    l_self = p[HALO:HALO + tt, IN_WIDTH:IN_WIDTH + LORA_BLOCK]
    l_nbr = p[:, IN_WIDTH + LORA_BLOCK:]
    lora = l_self + prev_rows(l_nbr) + next_rows(l_nbr)

    tw = jnp.tanh(lora[:, 0:128]).astype(BF16)
    zw = jnp.dot(tw, w2_ref[...], preferred_element_type=F32) + w0_ref[...]
    lw = -DECAY_SCALE * jax.nn.sigmoid(zw)
    za = jnp.dot(lora[:, 128:256].astype(BF16), a2_ref[...], preferred_element_type=F32) + a0_ref[...]
    aa = jax.nn.sigmoid(za)
    sg = jax.nn.sigmoid(lora[:, 256:512]).astype(BF16)
    g = jnp.dot(sg, g2_ref[...], preferred_element_type=F32)

    kk = k * kk_ref[...]
    kk = kk * lax.rsqrt(jnp.maximum(_seg_sum(kk * kk, seg_ref), 1e-24))
    k_a = ka_ref[...]
    kd0 = k * (1.0 + (aa[:, :RWKV_WIDTH] - 1.0) * k_a)
    kd1 = k * (1.0 + (aa[:, RWKV_WIDTH:] - 1.0) * k_a)
    bonus = _seg_sum(r * rk_ref[...] * (kd0 + kd1), seg_ref) * v

    o_r[...] = r
    o_v[...] = v
    o_kk[...] = kk
    o_lw[0] = lw[:, :RWKV_WIDTH]
    o_lw[1] = lw[:, RWKV_WIDTH:]
    o_kd[0] = kd0
    o_kd[1] = kd1
    o_b[0] = kk * aa[:, :RWKV_WIDTH]
    o_b[1] = kk * aa[:, RWKV_WIDTH:]
    o_g[...] = g
    o_bonus[...] = bonus

    cos_t = jnp.concatenate([cos_ref[...]] * 4, axis=1)
    sa_t = jnp.concatenate([sa_ref[...]] * 4, axis=1)
    sb_t = jnp.concatenate([sb_ref[...]] * 4, axis=1)

    def qk_norm_rope(xq, g_ref):
        ssq = _seg_sum(xq * xq, seg_ref) * (1.0 / DIFF_QK_DIM)
        xn = xq * lax.rsqrt(ssq + NORM_EPS) * g_ref[...]
        return xn * cos_t + _roll_lanes(xn, LANES - ROPE_HALF) * sa_t + _roll_lanes(xn, ROPE_HALF) * sb_t

    qd = p[HALO:HALO + tt, c0:c0 + DIFF_WIDTH]
    kd = p[HALO:HALO + tt, c0 + DIFF_WIDTH:c0 + 2 * DIFF_WIDTH]
    vd = p[HALO:HALO + tt, c0 + 2 * DIFF_WIDTH:c0 + 3 * DIFF_WIDTH]
    o_q[...] = (qk_norm_rope(qd, qg_ref) * (DIFF_QK_DIM ** -0.5)).astype(BF16)
    o_k[...] = qk_norm_rope(kd, kg_ref).astype(BF16)
    o_vd[0] = vd.T.astype(BF16)


def _pre_real_kernel(nt, tt, xm_ref, xp_ref, xn_ref, meta_ref, cos_ref, sa_ref, sb_ref, *rest):
    w, outs = rest[:14], rest[14:]
    j = pl.program_id(0) % nt
    xp = jnp.where(j == 0, meta_ref[N_META - HALO:N_META, :], xp_ref[...])
    xn = jnp.where(j == nt - 1, 0.0, xn_ref[...])
    xx = jnp.concatenate([xp, xm_ref[...], xn], axis=0)
    _pre_compute(xx, tt, (cos_ref, sa_ref, sb_ref), w, outs)


def _pre_meta_kernel(xn_ref, meta_ref, cos_ref, sa_ref, sb_ref, *rest):
    w, outs = rest[:14], rest[14:]
    xx = jnp.concatenate([jnp.zeros((HALO, D_MODEL), F32), meta_ref[...], xn_ref[0]], axis=0)
    _pre_compute(xx, N_META, (cos_ref, sa_ref, sb_ref), w, outs)


def _pre_out_shapes(batch, seq):
    rows = batch * seq
    one = jax.ShapeDtypeStruct((rows, RWKV_WIDTH), F32)
    two = jax.ShapeDtypeStruct((2, rows, RWKV_WIDTH), F32)
    half = jax.ShapeDtypeStruct((rows, DIFF_WIDTH), BF16)
    v_t = jax.ShapeDtypeStruct((batch, DIFF_WIDTH, seq), BF16)
    return (one, one, one, two, two, two, one, one, half, half, v_t)


def _pre_out_specs(tt, nt):
    one = pl.BlockSpec((tt, RWKV_WIDTH), lambda i: (i, 0))
    two = pl.BlockSpec((2, tt, RWKV_WIDTH), lambda i: (0, i, 0))
    v_t = pl.BlockSpec((1, DIFF_WIDTH, tt), lambda i: (i // nt, 0, i % nt))
    return (one, one, one, two, two, two, one, one, one, one, v_t)


def _pre_weight_specs():
    shapes = [(1, D_MODEL), (D_MODEL, CAT_WIDTH), (3, 3 * RWKV_WIDTH), (1, 2 * RWKV_WIDTH),
              (2 * DECAY_LORA, 2 * RWKV_WIDTH), (1, 2 * RWKV_WIDTH), (2 * AAA_LORA, 2 * RWKV_WIDTH),
              (GATE_LORA_PAD, RWKV_WIDTH), (1, RWKV_WIDTH), (1, RWKV_WIDTH), (1, RWKV_WIDTH),
              (1, DIFF_WIDTH), (1, DIFF_WIDTH), (256, 256)]
    return [_const_spec(s) for s in shapes]


def _pre_real(x2, meta, tabs_real, weights, seq, tt):
    rows = x2.shape[0]
    nt = seq // tt
    hb = tt // HALO
    last_halo = rows // HALO - 1
    in_specs = [
        pl.BlockSpec((tt, D_MODEL), lambda i: (i, 0)),
        pl.BlockSpec((HALO, D_MODEL), lambda i: (jnp.maximum(i * hb - 1, 0), 0)),
        pl.BlockSpec((HALO, D_MODEL), lambda i: (jnp.minimum((i + 1) * hb, last_halo), 0)),
        _const_spec((N_META, D_MODEL)),
    ] + [pl.BlockSpec((tt, LANES), lambda i: (i % nt, 0))] * 3 + _pre_weight_specs()
    return pl.pallas_call(
        functools.partial(_pre_real_kernel, nt, tt),
        grid=(rows // tt,),
        in_specs=in_specs,
        out_specs=_pre_out_specs(tt, nt),
        out_shape=_pre_out_shapes(rows // seq, seq),
        compiler_params=pltpu.CompilerParams(dimension_semantics=("parallel",), vmem_limit_bytes=VMEM_LIMIT),
        name="pre_real",
    )(x2, x2, x2, meta, *tabs_real, *weights)


def _pre_meta(x3, meta, tabs_meta, weights):
    batch = x3.shape[0]
    in_specs = [
        pl.BlockSpec((1, HALO, D_MODEL), lambda b: (b, 0, 0)),
        _const_spec((N_META, D_MODEL)),
    ] + [_const_spec((N_META, LANES))] * 3 + _pre_weight_specs()
    return pl.pallas_call(
        _pre_meta_kernel,
        grid=(batch,),
        in_specs=in_specs,
        out_specs=_pre_out_specs(N_META, 1),
        out_shape=_pre_out_shapes(batch, N_META),
        compiler_params=pltpu.CompilerParams(dimension_semantics=("parallel",), vmem_limit_bytes=VMEM_LIMIT),
        name="pre_meta",
    )(x3, meta, *tabs_meta, *weights)


def _scan_kernel(ncr, bb, *refs):
    fwd_refs, bwd_refs, meta_refs = refs[0:6], refs[6:12], refs[12:18]
    yf_ref, yb_ref, s_ref = refs[18:21]
    c = pl.program_id(1)
    is_meta = c == 0

    @pl.when(c == 0)
    def _():
        s_ref[...] = jnp.zeros_like(s_ref)

    pad = jnp.zeros((CHUNK - N_META, RWKV_WIDTH), F32)
    row = lax.broadcasted_iota(jnp.int32, (CHUNK, CHUNK), 0)
    colc = lax.broadcasted_iota(jnp.int32, (CHUNK, CHUNK), 1)
    rowt = lax.broadcasted_iota(jnp.int32, (CHUNK, LANES), 0)
    lane = lax.broadcasted_iota(jnp.int32, (CHUNK, LANES), 1)
    src = lane % CHUNK
    left = lane < HEAD_DIM
    right = jnp.logical_not(left)
    m_incl = {True: (colc <= row).astype(BF16), False: (colc >= row).astype(BF16)}
    strict = {True: src < rowt, False: src > rowt}
    incl = {True: src <= rowt, False: src >= rowt}

    def load(refs6, j, fwd):
        vals = []
        for idx, ref in enumerate(refs6):
            x = ref[j] if idx < 3 else ref[0, j]
            if fwd:
                mref = meta_refs[idx]
                meta = mref[j] if idx < 3 else mref[0, j]
                x = jnp.where(is_meta, jnp.concatenate([meta, pad], axis=0), x)
            vals.append(x)
        return vals

    streams = [(load(fwd_refs, j, True), True) for j in range(bb)]
    streams += [(load(bwd_refs, j, False), False) for j in range(bb)]

    prep = []
    for (r, v, kk, lw, kd, b), fwd in streams:
        lw_hi = lw.astype(BF16)
        lw_lo = (lw - lw_hi.astype(F32)).astype(BF16)
        ci = (jnp.dot(m_incl[fwd], lw_hi, preferred_element_type=F32)
              + jnp.dot(m_incl[fwd], lw_lo, preferred_element_type=F32))
        tot = jnp.sum(lw, axis=0, keepdims=True)
        e_inv = jnp.exp(-ci)
        e_rest = jnp.exp(tot - ci)
        prep.append(dict(
            fwd=fwd, gam=jnp.exp(tot), rt=r * jnp.exp(ci), at=-kk * jnp.exp(ci - lw), kt=kd * e_inv,
            bt=b * e_inv, kh=kd * e_rest, bh=b * e_rest,
            v_sw=[pltpu.roll(v[:, LANES * pr:LANES * (pr + 1)], HEAD_DIM, 1) for pr in range(RWKV_HEADS // 2)]))

    items = [(si, h) for si in range(len(streams)) for h in range(RWKV_HEADS)]

    def pair(x, h):
        return x[:, LANES * (h // 2):LANES * (h // 2 + 1)]

    def nat(h):
        return left if h % 2 == 0 else right

    def oth(h):
        return right if h % 2 == 0 else left

    def stage(fn):
        return {it: fn(*it) for it in items}

    rhs_g = {(si, pr): jnp.concatenate([pair(p['bt'], 2 * pr), pair(p['kt'], 2 * pr)], axis=0).astype(BF16)
             for si, p in enumerate(prep) for pr in range(RWKV_HEADS // 2)}
    rhs_v = {(si, pr): jnp.concatenate([p['v_sw'][pr]] * 2, axis=0).astype(BF16)
             for si, p in enumerate(prep) for pr in range(RWKV_HEADS // 2)}
    at_n = stage(lambda si, h: jnp.where(nat(h), pair(prep[si]['at'], h), 0.0))
    contract = (((1,), (1,)), ((), ()))
    gmat = stage(lambda si, h: lax.dot_general(
        jnp.concatenate([at_n[si, h], jnp.where(nat(h), pair(prep[si]['rt'], h), 0.0)], axis=0).astype(BF16),
        rhs_g[si, h // 2], contract, preferred_element_type=F32))
    pmat = stage(lambda si, h: jnp.where(jnp.logical_and(strict[prep[si]['fwd']], left),
                                         gmat[si, h][:CHUNK], 0.0))
    akv = stage(lambda si, h: jnp.dot(
        jnp.where(jnp.logical_and(strict[prep[si]['fwd']], right), gmat[si, h][:CHUNK], 0.0).astype(BF16),
        rhs_v[si, h // 2], preferred_element_type=F32))
    z = stage(lambda si, h: at_n[si, h] + jnp.where(oth(h), akv[si, h], 0.0))
    for _ in range(5):
        res = stage(lambda si, h: jnp.dot(
            pmat[si, h][:, :CHUNK].astype(BF16),
            jnp.concatenate([z[si, h], pmat[si, h]], axis=1).astype(BF16), preferred_element_type=F32))
        z = stage(lambda si, h: z[si, h] + res[si, h][:, :LANES])
        pmat = stage(lambda si, h: res[si, h][:, LANES:])
    zf = stage(lambda si, h: jnp.dot(pmat[si, h][:, :CHUNK].astype(BF16), z[si, h].astype(BF16),
                                     preferred_element_type=F32))
    rhs2 = stage(lambda si, h: jnp.concatenate(
        [z[si, h] + zf[si, h], jnp.where(oth(h), prep[si]['v_sw'][h // 2], 0.0)], axis=0).astype(BF16))
    out_top = stage(lambda si, h: jnp.dot(
        jnp.where(incl[prep[si]['fwd']], gmat[si, h][CHUNK:], 0.0).astype(BF16), rhs2[si, h],
        preferred_element_type=F32))

    def bk_t(si, h):
        bk = jnp.concatenate([jnp.where(nat(h), pair(prep[si]['bh'], h), 0.0),
                              jnp.where(nat(h), pair(prep[si]['kh'], h), 0.0)], axis=0)
        return bk.T[HEAD_DIM * (h % 2):HEAD_DIM * (h % 2 + 1)].astype(BF16)

    out_bot = stage(lambda si, h: jnp.dot(bk_t(si, h), rhs2[si, h], preferred_element_type=F32))

    def m_rows(si, h):
        diag = jnp.where(lane == rowt + HEAD_DIM * (h % 2), pair(prep[si]['gam'], h), 0.0)
        return jnp.where(nat(h), out_bot[si, h], 0.0) + diag

    new_state = {}
    y_out = {}
    for si, p in enumerate(prep):
        for pr in range(RWKV_HEADS // 2):
            h0, h1 = 2 * pr, 2 * pr + 1
            r_acc = (jnp.where(left, out_top[si, h0] + pair(p['rt'], h0), 0.0)
                     + jnp.where(right, out_top[si, h1] + pair(p['rt'], h1), 0.0))
            y_acc = jnp.where(right, out_top[si, h0], 0.0) + jnp.where(left, out_top[si, h1], 0.0)
            m_pair = jnp.concatenate([m_rows(si, h0), m_rows(si, h1)], axis=0).astype(BF16)
            n_pair = jnp.concatenate([jnp.where(right, out_bot[si, h0], 0.0),
                                      jnp.where(left, out_bot[si, h1], 0.0)], axis=0)
            s_b = s_ref[si, pr].astype(BF16)
            y_sw = jnp.dot(r_acc.astype(BF16), s_b, preferred_element_type=F32) + y_acc
            new_state[si, pr] = jnp.dot(m_pair, s_b, preferred_element_type=F32) + n_pair
            y_out[si, pr] = pltpu.roll(y_sw, HEAD_DIM, 1)

    for j in range(bb):
        for pr in range(RWKV_HEADS // 2):
            s_ref[j, pr] = new_state[j, pr]
            yf_ref[j, :, LANES * pr:LANES * (pr + 1)] = y_out[j, pr]

    @pl.when(c < ncr)
    def _():
        for j in range(bb):
            for pr in range(RWKV_HEADS // 2):
                s_ref[bb + j, pr] = new_state[bb + j, pr]
                yb_ref[j, :, LANES * pr:LANES * (pr + 1)] = y_out[bb + j, pr]


def _scan(real, meta, batch, seq, bb):
    ncr = seq // CHUNK
    assert batch % bb == 0

    def fidx(c):
        return jnp.maximum(c - 1, 0)

    def bidx(c):
        return jnp.maximum(ncr - 1 - c, 0)

    def specs(idx, d):
        one = pl.BlockSpec((bb, CHUNK, RWKV_WIDTH), lambda g, c: (g, idx(c), 0))
        two = pl.BlockSpec((1, bb, CHUNK, RWKV_WIDTH), lambda g, c: (d, g, idx(c), 0))
        return [one, one, one, two, two, two]

    mone = pl.BlockSpec((bb, N_META, RWKV_WIDTH), lambda g, c: (g, 0, 0))
    mtwo = pl.BlockSpec((1, bb, N_META, RWKV_WIDTH), lambda g, c: (0, g, 0, 0))
    y_shape = jax.ShapeDtypeStruct((batch, seq, RWKV_WIDTH), F32)
    return pl.pallas_call(
        functools.partial(_scan_kernel, ncr, bb),
        grid=(batch // bb, ncr + 1),
        in_specs=specs(fidx, 0) + specs(bidx, 1) + [mone, mone, mone, mtwo, mtwo, mtwo],
        out_specs=(pl.BlockSpec((bb, CHUNK, RWKV_WIDTH), lambda g, c: (g, fidx(c), 0)),
                   pl.BlockSpec((bb, CHUNK, RWKV_WIDTH), lambda g, c: (g, bidx(c), 0))),
        out_shape=(y_shape, y_shape),
        scratch_shapes=[pltpu.VMEM((2 * bb, RWKV_HEADS // 2, LANES, LANES), F32)],
        compiler_params=pltpu.CompilerParams(
            dimension_semantics=("parallel", "arbitrary"), vmem_limit_bytes=VMEM_LIMIT),
        name="wkv_scan",
    )(*real, *real, *meta)


def _attn_kernel(nkb, tk, group, q_ref, k_ref, vt_ref, km_ref, vmt_ref, lam_ref, sg_ref, o_ref):
    q = q_ref[0]
    tq = q.shape[0]
    lane = lax.broadcasted_iota(jnp.int32, (tq, LANES), 1)
    zero = jnp.zeros_like(q)
    qc = (jnp.where(lane < DIFF_QK_DIM, q, zero), jnp.where(lane >= DIFF_QK_DIM, q, zero))
    contract = (((1,), (1,)), ((), ()))

    def blocks(kbs, vtbs, state):
        scores = [[lax.dot_general(kb, qc[comp], contract, preferred_element_type=F32) for comp in range(2)]
                  for kb in kbs]
        state = list(state)
        for s_pair, vtb in zip(scores, vtbs):
            for comp in range(2):
                m, l, acc = state[comp]
                s = s_pair[comp]
                m_new = jnp.maximum(m, jnp.max(s, axis=0, keepdims=True))
                alpha = jnp.exp(m - m_new)
                pexp = jnp.exp(s - m_new)
                l_new = alpha * l + jnp.sum(pexp, axis=0, keepdims=True)
                acc_new = alpha * acc + jnp.dot(vtb, pexp.astype(BF16), preferred_element_type=F32)
                state[comp] = (m_new, l_new, acc_new)
        return tuple(state)

    init = tuple((jnp.full((1, tq), -jnp.inf, F32), jnp.zeros((1, tq), F32), jnp.zeros((LANES, tq), F32))
                 for _ in range(2))
    state = blocks([km_ref[0]], [vmt_ref[0]], init)

    def body(i, st):
        starts = [pl.multiple_of((i * group + j) * tk, tk) for j in range(group)]
        return blocks([k_ref[0, pl.ds(st_j, tk), :] for st_j in starts],
                      [vt_ref[0, :, pl.ds(st_j, tk)] for st_j in starts], st)

    state = lax.fori_loop(0, nkb // group, body, state)
    lam_v = lam_ref[...]
    lam = (jnp.exp(jnp.sum(lam_v[0:1] * lam_v[1:2], axis=-1, keepdims=True))
           - jnp.exp(jnp.sum(lam_v[2:3] * lam_v[3:4], axis=-1, keepdims=True)) + LAMBDA_INIT)
    (_, l0, a0), (_, l1, a1) = state
    o = a0 / l0 - lam * (a1 / l1)
    o = o * lax.rsqrt(jnp.mean(o * o, axis=0, keepdims=True) + NORM_EPS) * sg_ref[...]
    o_ref[0] = (o * (1.0 - LAMBDA_INIT)).T


def _attention(q, k, vt, km, vmt, lam, subln, tq, tk):
    batch, seq, _ = q.shape
    nkb = seq // tk
    return pl.pallas_call(
        functools.partial(_attn_kernel, nkb, tk, math.gcd(nkb, ATTN_GROUP)),
        grid=(batch, DIFF_HEADS, seq // tq),
        in_specs=[
            pl.BlockSpec((1, tq, LANES), lambda b, h, i: (b, i, h)),
            pl.BlockSpec((1, seq, LANES), lambda b, h, i: (b, 0, h)),
            pl.BlockSpec((1, LANES, seq), lambda b, h, i: (b, h, 0)),
            pl.BlockSpec((1, N_META, LANES), lambda b, h, i: (b, 0, h)),
            pl.BlockSpec((1, LANES, N_META), lambda b, h, i: (b, h, 0)),
            _const_spec((4, DIFF_QK_DIM)),
            _const_spec((DIFF_V_DIM, 1)),
        ],
        out_specs=pl.BlockSpec((1, tq, LANES), lambda b, h, i: (b, i, h)),
        out_shape=jax.ShapeDtypeStruct((batch, seq, DIFF_WIDTH), F32),
        compiler_params=pltpu.CompilerParams(
            dimension_semantics=("parallel", "parallel", "parallel"), vmem_limit_bytes=VMEM_LIMIT),
        name="diff_attn",
    )(q, k, vt, km, vmt, lam, subln)


def _post_kernel(x_ref, yf_ref, ybw_ref, bonus_ref, g_ref, yb_ref, lng_ref, lnb_ref, seg_ref, wout_ref, ln2_ref,
                 wg_ref, wu_ref, wd_ref, o_ref):
    ys = yf_ref[...] + ybw_ref[...]
    inv_n = 1.0 / HEAD_DIM
    mu = _seg_sum(ys, seg_ref) * inv_n
    dev = ys - mu
    var = _seg_sum(dev * dev, seg_ref) * inv_n
    yn = dev * lax.rsqrt(var + GN_EPS) * lng_ref[...] + lnb_ref[...]
    ya = (yn + bonus_ref[...]) * g_ref[...]
    mix = jnp.concatenate([ya, yb_ref[...]], axis=1).astype(BF16)
    x1 = x_ref[...] + jnp.dot(mix, wout_ref[...], preferred_element_type=F32)
    n2 = x1 * lax.rsqrt(jnp.mean(x1 * x1, axis=-1, keepdims=True) + NORM_EPS) * ln2_ref[...]
    n2 = n2.astype(BF16)
    gate = jnp.dot(n2, wg_ref[...], preferred_element_type=F32)
    up = jnp.dot(n2, wu_ref[...], preferred_element_type=F32)
    hid = (gate * jax.nn.sigmoid(gate) * up).astype(BF16)
    o_ref[...] = x1 + jnp.dot(hid, wd_ref[...], preferred_element_type=F32)


def _post(x2, y_f, y_b, bonus, g, yb, weights, tt):
    rows = x2.shape[0]
    row512 = pl.BlockSpec((tt, RWKV_WIDTH), lambda i: (i, 0))
    in_specs = [
        pl.BlockSpec((tt, D_MODEL), lambda i: (i, 0)),
        row512, row512, row512, row512, row512,
        _const_spec((1, RWKV_WIDTH)), _const_spec((1, RWKV_WIDTH)), _const_spec((256, 256)),
        _const_spec((D_MODEL, D_MODEL)), _const_spec((1, D_MODEL)),
        _const_spec((D_MODEL, D_FF)), _const_spec((D_MODEL, D_FF)), _const_spec((D_FF, D_MODEL)),
    ]
    return pl.pallas_call(
        _post_kernel,
        grid=(rows // tt,),
        in_specs=in_specs,
        out_specs=pl.BlockSpec((tt, D_MODEL), lambda i: (i, 0)),
        out_shape=jax.ShapeDtypeStruct((rows, D_MODEL), F32),
        compiler_params=pltpu.CompilerParams(dimension_semantics=("parallel",), vmem_limit_bytes=VMEM_LIMIT),
        name="post",
    )(x2, y_f, y_b, bonus, g, yb, *weights)


def _rope_tables(total):
    freqs = ROPE_THETA ** (-jnp.arange(0, ROPE_DIMS, 2, dtype=F32) / ROPE_DIMS)
    ang = jnp.arange(total, dtype=F32)[:, None] * freqs[None, :]
    cos, sin = jnp.cos(ang), jnp.sin(ang)
    ones = jnp.ones((total, DIFF_QK_DIM - ROPE_DIMS), F32)
    zeros = jnp.zeros((total, ROPE_HALF), F32)
    zrest = jnp.zeros((total, DIFF_QK_DIM - ROPE_DIMS), F32)
    cos64 = jnp.concatenate([cos, cos, ones], axis=1)
    sa64 = jnp.concatenate([-sin, zeros, zrest], axis=1)
    sb64 = jnp.concatenate([zeros, sin, zrest], axis=1)
    return tuple(jnp.concatenate([t, t], axis=1) for t in (cos64, sa64, sb64))


def _prepare_weights(p):
    def lora_cols(scale_fn):
        cols = [scale_fn(p['mix_w'][0])[:, None] * p['decay_w1'][0],
                scale_fn(p['mix_w'][1])[:, None] * p['decay_w1'][1],
                scale_fn(p['mix_a'][0])[:, None] * p['aaa_a1'][0],
                scale_fn(p['mix_a'][1])[:, None] * p['aaa_a1'][1],
                scale_fn(p['mix_g'])[:, None] * p['gate_g1'],
                jnp.zeros((D_MODEL, GATE_LORA_PAD - GATE_LORA), F32)]
        return jnp.concatenate(cols, axis=1)

    wcat = jnp.concatenate([p['w_in'], lora_cols(lambda m: 1.0 - m), lora_cols(lambda m: 0.5 * m)], axis=1)

    def blockdiag(w):
        z = jnp.zeros_like(w[0])
        return jnp.concatenate([jnp.concatenate([w[0], z], axis=1), jnp.concatenate([z, w[1]], axis=1)], axis=0)

    g2 = jnp.concatenate([p['gate_g2'], jnp.zeros((GATE_LORA_PAD - GATE_LORA, RWKV_WIDTH), F32)], axis=0)
    seg = (jnp.arange(256)[:, None] // HEAD_DIM == jnp.arange(256)[None, :] // HEAD_DIM).astype(BF16)
    pre_w = (
        p['ln1_g'][None, :], wcat.astype(BF16), p['conv_rkv'],
        p['decay_w0'].reshape(1, 2 * RWKV_WIDTH), blockdiag(p['decay_w2']).astype(BF16),
        p['aaa_a0'].reshape(1, 2 * RWKV_WIDTH), blockdiag(p['aaa_a2']).astype(BF16),
        g2.astype(BF16), p['k_k'][None, :], p['k_a'][None, :], p['r_k'].reshape(1, RWKV_WIDTH),
        jnp.tile(p['q_norm_g'], DIFF_WIDTH // DIFF_QK_DIM)[None, :],
        jnp.tile(p['k_norm_g'], DIFF_WIDTH // DIFF_QK_DIM)[None, :], seg,
    )
    post_w = (
        p['lnx_g'][None, :], p['lnx_b'][None, :], seg, p['w_out'].astype(BF16), p['ln2_g'][None, :],
        p['w_gate'].astype(BF16), p['w_up'].astype(BF16), p['w_down'].astype(BF16),
    )
    return pre_w, post_w


def _tile_rows(seq, cap):
    tt = min(seq, cap)
    assert seq % tt == 0 and tt % CHUNK == 0
    return tt


def _trunk(x, meta, pre_w, post_w, lam, subln, tt_pre, tt_post, tq, tk, scan_bb):
    batch, seq, _ = x.shape
    rows = batch * seq
    x2 = x.reshape(rows, D_MODEL)
    tabs = _rope_tables(N_META + seq)
    tabs_meta = tuple(t[:N_META] for t in tabs)
    tabs_real = tuple(t[N_META:] for t in tabs)

    real = _pre_real(x2, meta, tabs_real, pre_w, seq, _tile_rows(seq, tt_pre))
    metao = _pre_meta(x, meta, tabs_meta, pre_w)
    (r, v, kk, lw, kd, b, g, bonus, q, k, vd) = real
    (mr, mv, mkk, mlw, mkd, mb, _, _, _, mk, mvd) = metao

    def r3(a, n):
        return a.reshape(a.shape[:-2] + (batch, n, a.shape[-1]))

    y_f, y_b = _scan((r3(r, seq), r3(v, seq), r3(kk, seq), r3(lw, seq), r3(kd, seq), r3(b, seq)),
                     (r3(mr, N_META), r3(mv, N_META), r3(mkk, N_META), r3(mlw, N_META), r3(mkd, N_META),
                      r3(mb, N_META)), batch, seq, scan_bb)
    yb = _attention(r3(q, seq), r3(k, seq), vd, r3(mk, N_META), mvd, lam, subln,
                    _tile_rows(seq, tq), _tile_rows(seq, tk))
    out = _post(x2, y_f.reshape(rows, RWKV_WIDTH), y_b.reshape(rows, RWKV_WIDTH), bonus, g,
                yb.reshape(rows, DIFF_WIDTH), post_w,
                _tile_rows(seq, tt_post))
    return out.reshape(batch, seq, D_MODEL)


def kernel(x_prompt, x_sample, meta_tokens, ln1_g, w_in, conv_rkv, mix_w, mix_a, mix_g, decay_w0, decay_w1,
           decay_w2, aaa_a0, aaa_a1, aaa_a2, gate_g1, gate_g2, k_k, k_a, r_k, lnx_g, lnx_b, q_norm_g, k_norm_g,
           diff_lambdas, subln_g, w_out, ln2_g, w_gate, w_up, w_down):
    params = dict(ln1_g=ln1_g, w_in=w_in, conv_rkv=conv_rkv, mix_w=mix_w, mix_a=mix_a, mix_g=mix_g,
                  decay_w0=decay_w0, decay_w1=decay_w1, decay_w2=decay_w2, aaa_a0=aaa_a0, aaa_a1=aaa_a1,
                  aaa_a2=aaa_a2, gate_g1=gate_g1, gate_g2=gate_g2, k_k=k_k, k_a=k_a, r_k=r_k, lnx_g=lnx_g,
                  lnx_b=lnx_b, q_norm_g=q_norm_g, k_norm_g=k_norm_g, w_out=w_out, ln2_g=ln2_g, w_gate=w_gate,
                  w_up=w_up, w_down=w_down)
    p = {name: arr[0] for name, arr in params.items()}
    pre_w, post_w = _prepare_weights(p)
    lam = diff_lambdas[0]
    subln = subln_g[0][:, None]
    outs = []
    for x in (x_prompt, x_sample):
        outs.append(_trunk(x, meta_tokens, pre_w, post_w, lam, subln, 256, 256, 256, 512, 1))
    return tuple(outs)
```

```python
import functools
import math

import jax
import jax.numpy as jnp
from jax import lax
from jax.experimental import pallas as pl
from jax.experimental.pallas import tpu as pltpu

F32 = jnp.float32
BF16 = jnp.bfloat16

D_MODEL = 1024
N_META = 16
RWKV_HEADS = 8
HEAD_DIM = 64
RWKV_WIDTH = 512
DIFF_HEADS = 4
DIFF_QK_DIM = 64
DIFF_V_DIM = 128
DIFF_WIDTH = 512
IN_WIDTH = 3072
DECAY_LORA = 64
AAA_LORA = 64
GATE_LORA = 160
GATE_LORA_PAD = 256
LORA_BLOCK = 512
CAT_WIDTH = IN_WIDTH + 2 * LORA_BLOCK
ROPE_THETA = 500000.0
ROPE_DIMS = 16
ROPE_HALF = 8
D_FF = 2816
NORM_EPS = 1e-6
GN_EPS = 64e-5
DECAY_SCALE = math.exp(-0.5)
LAMBDA_INIT = 0.8 - 0.6 * math.exp(0.0)

LANES = 128
SUBLANES = 8
HALO = SUBLANES
CHUNK = 64
SCAN_GROUP = 4
ATTN_GROUP = 8
VT_ROWS = DIFF_V_DIM + 16
LOG2E = math.log2(math.e)
VMEM_LIMIT = 56 * 1024 * 1024


def _const_spec(shape):
    zeros = (0,) * len(shape)
    return pl.BlockSpec(shape, lambda *_: zeros)


def _seg_sum(x, seg_ref):
    seg = seg_ref[...]
    parts = []
    for c in range(x.shape[1] // 256):
        xs = x[:, 256 * c:256 * (c + 1)].astype(BF16)
        parts.append(jnp.dot(xs, seg, preferred_element_type=F32))
    return jnp.concatenate(parts, axis=1)


def _roll_lanes(x, shift):
    parts = [pltpu.roll(x[:, LANES * c:LANES * (c + 1)], shift, 1) for c in range(x.shape[1] // LANES)]
    return jnp.concatenate(parts, axis=1)


def _pre_compute(xx, tt, tabs, w, outs):
    (cos_ref, sa_ref, sb_ref) = tabs
    (ln1_ref, wcat_ref, conv_ref, w0_ref, w2_ref, a0_ref, a2_ref, g2_ref, kk_ref, ka_ref, rk_ref,
     qg_ref, kg_ref, seg_ref) = w
    (o_r, o_v, o_kk, o_lw, o_kd, o_b, o_g, o_bonus, o_q, o_k, o_vd) = outs
    rows = tt + 2 * HALO

    ms = jnp.mean(xx * xx, axis=-1, keepdims=True)
    n = xx * lax.rsqrt(ms + NORM_EPS) * ln1_ref[...]
    p = jnp.dot(n.astype(BF16), wcat_ref[...], preferred_element_type=F32)

    def prev_rows(a):
        return pltpu.roll(a, 1, 0)[HALO:HALO + tt]

    def next_rows(a):
        return pltpu.roll(a, rows - 1, 0)[HALO:HALO + tt]

    c0 = 3 * RWKV_WIDTH
    rkv_in = p[:, :c0]
    conv = conv_ref[...]
    rkv = conv[0:1] * prev_rows(rkv_in) + conv[1:2] * rkv_in[HALO:HALO + tt] + conv[2:3] * next_rows(rkv_in)
    r = rkv[:, :RWKV_WIDTH]
    k = rkv[:, RWKV_WIDTH:2 * RWKV_WIDTH]
    v = rkv[:, 2 * RWKV_WIDTH:]

    l_self = p[HALO:HALO + tt, IN_WIDTH:IN_WIDTH + LORA_BLOCK]
    l_nbr = p[:, IN_WIDTH + LORA_BLOCK:]
    lora = l_self + prev_rows(l_nbr) + next_rows(l_nbr)

    tw = jnp.tanh(lora[:, 0:128]).astype(BF16)
    zw = jnp.dot(tw, w2_ref[...], preferred_element_type=F32) + w0_ref[...]
    lw = -DECAY_SCALE * jax.nn.sigmoid(zw)
    za = jnp.dot(lora[:, 128:256].astype(BF16), a2_ref[...], preferred_element_type=F32) + a0_ref[...]
    aa = jax.nn.sigmoid(za)
    sg = jax.nn.sigmoid(lora[:, 256:512]).astype(BF16)
    g = jnp.dot(sg, g2_ref[...], preferred_element_type=F32)

    kk = k * kk_ref[...]
    kk = kk * lax.rsqrt(jnp.maximum(_seg_sum(kk * kk, seg_ref), 1e-24))
    k_a = ka_ref[...]
    kd0 = k * (1.0 + (aa[:, :RWKV_WIDTH] - 1.0) * k_a)
    kd1 = k * (1.0 + (aa[:, RWKV_WIDTH:] - 1.0) * k_a)
    bonus = _seg_sum(r * rk_ref[...] * (kd0 + kd1), seg_ref) * v

    o_r[...] = r
    o_v[...] = v
    o_kk[...] = kk
    o_lw[0] = lw[:, :RWKV_WIDTH]
    o_lw[1] = lw[:, RWKV_WIDTH:]
    o_kd[0] = kd0
    o_kd[1] = kd1
    o_b[0] = kk * aa[:, :RWKV_WIDTH]
    o_b[1] = kk * aa[:, RWKV_WIDTH:]
    o_g[...] = g
    o_bonus[...] = bonus

    cos_t = jnp.concatenate([cos_ref[...]] * 4, axis=1)
    sa_t = jnp.concatenate([sa_ref[...]] * 4, axis=1)
    sb_t = jnp.concatenate([sb_ref[...]] * 4, axis=1)

    def qk_norm_rope(xq, g_ref):
        ssq = _seg_sum(xq * xq, seg_ref) * (1.0 / DIFF_QK_DIM)
        xn = xq * lax.rsqrt(ssq + NORM_EPS) * g_ref[...]
        return xn * cos_t + _roll_lanes(xn, LANES - ROPE_HALF) * sa_t + _roll_lanes(xn, ROPE_HALF) * sb_t

    qd = p[HALO:HALO + tt, c0:c0 + DIFF_WIDTH]
    kd = p[HALO:HALO + tt, c0 + DIFF_WIDTH:c0 + 2 * DIFF_WIDTH]
    vd = p[HALO:HALO + tt, c0 + 2 * DIFF_WIDTH:c0 + 3 * DIFF_WIDTH]
    o_q[...] = (qk_norm_rope(qd, qg_ref) * (DIFF_QK_DIM ** -0.5 * LOG2E)).astype(BF16)
    o_k[...] = qk_norm_rope(kd, kg_ref).astype(BF16)
    vd_t = vd.T
    fill = (lax.broadcasted_iota(jnp.int32, (VT_ROWS - DIFF_V_DIM, tt), 0) == 0).astype(F32)
    o_vd[0] = jnp.concatenate(
        [piece for h in range(DIFF_HEADS) for piece in (vd_t[DIFF_V_DIM * h:DIFF_V_DIM * (h + 1)], fill)],
        axis=0).astype(BF16)


def _pre_real_kernel(nt, tt, xm_ref, xp_ref, xn_ref, meta_ref, cos_ref, sa_ref, sb_ref, *rest):
    w, outs = rest[:14], rest[14:]
    j = pl.program_id(0) % nt
    xp = jnp.where(j == 0, meta_ref[N_META - HALO:N_META, :], xp_ref[...])
    xn = jnp.where(j == nt - 1, 0.0, xn_ref[...])
    xx = jnp.concatenate([xp, xm_ref[...], xn], axis=0)
    _pre_compute(xx, tt, (cos_ref, sa_ref, sb_ref), w, outs)


def _pre_meta_kernel(xn_ref, meta_ref, cos_ref, sa_ref, sb_ref, *rest):
    w, outs = rest[:14], rest[14:]
    xx = jnp.concatenate([jnp.zeros((HALO, D_MODEL), F32), meta_ref[...], xn_ref[0]], axis=0)
    _pre_compute(xx, N_META, (cos_ref, sa_ref, sb_ref), w, outs)


def _pre_out_shapes(batch, seq):
    rows = batch * seq
    one = jax.ShapeDtypeStruct((rows, RWKV_WIDTH), F32)
    two = jax.ShapeDtypeStruct((2, rows, RWKV_WIDTH), F32)
    half = jax.ShapeDtypeStruct((rows, DIFF_WIDTH), BF16)
    v_t = jax.ShapeDtypeStruct((batch, DIFF_HEADS * VT_ROWS, seq), BF16)
    return (one, one, one, two, two, two, one, one, half, half, v_t)


def _pre_out_specs(tt, nt):
    one = pl.BlockSpec((tt, RWKV_WIDTH), lambda i: (i, 0))
    two = pl.BlockSpec((2, tt, RWKV_WIDTH), lambda i: (0, i, 0))
    v_t = pl.BlockSpec((1, DIFF_HEADS * VT_ROWS, tt), lambda i: (i // nt, 0, i % nt))
    return (one, one, one, two, two, two, one, one, one, one, v_t)


def _pre_weight_specs():
    shapes = [(1, D_MODEL), (D_MODEL, CAT_WIDTH), (3, 3 * RWKV_WIDTH), (1, 2 * RWKV_WIDTH),
              (2 * DECAY_LORA, 2 * RWKV_WIDTH), (1, 2 * RWKV_WIDTH), (2 * AAA_LORA, 2 * RWKV_WIDTH),
              (GATE_LORA_PAD, RWKV_WIDTH), (1, RWKV_WIDTH), (1, RWKV_WIDTH), (1, RWKV_WIDTH),
              (1, DIFF_WIDTH), (1, DIFF_WIDTH), (256, 256)]
    return [_const_spec(s) for s in shapes]


def _pre_real(x2, meta, tabs_real, weights, seq, tt):
    rows = x2.shape[0]
    nt = seq // tt
    hb = tt // HALO
    last_halo = rows // HALO - 1
    in_specs = [
        pl.BlockSpec((tt, D_MODEL), lambda i: (i, 0)),
        pl.BlockSpec((HALO, D_MODEL), lambda i: (jnp.maximum(i * hb - 1, 0), 0)),
        pl.BlockSpec((HALO, D_MODEL), lambda i: (jnp.minimum((i + 1) * hb, last_halo), 0)),
        _const_spec((N_META, D_MODEL)),
    ] + [pl.BlockSpec((tt, LANES), lambda i: (i % nt, 0))] * 3 + _pre_weight_specs()
    return pl.pallas_call(
        functools.partial(_pre_real_kernel, nt, tt),
        grid=(rows // tt,),
        in_specs=in_specs,
        out_specs=_pre_out_specs(tt, nt),
        out_shape=_pre_out_shapes(rows // seq, seq),
        compiler_params=pltpu.CompilerParams(dimension_semantics=("parallel",), vmem_limit_bytes=VMEM_LIMIT),
        name="pre_real",
    )(x2, x2, x2, meta, *tabs_real, *weights)


def _pre_meta(x3, meta, tabs_meta, weights):
    batch = x3.shape[0]
    in_specs = [
        pl.BlockSpec((1, HALO, D_MODEL), lambda b: (b, 0, 0)),
        _const_spec((N_META, D_MODEL)),
    ] + [_const_spec((N_META, LANES))] * 3 + _pre_weight_specs()
    return pl.pallas_call(
        _pre_meta_kernel,
        grid=(batch,),
        in_specs=in_specs,
        out_specs=_pre_out_specs(N_META, 1),
        out_shape=_pre_out_shapes(batch, N_META),
        compiler_params=pltpu.CompilerParams(dimension_semantics=("parallel",), vmem_limit_bytes=VMEM_LIMIT),
        name="pre_meta",
    )(x3, meta, *tabs_meta, *weights)


def _scan_kernel(ncr, bb, *refs):
    fwd_refs, bwd_refs, meta_refs = refs[0:6], refs[6:12], refs[12:18]
    yf_ref, yb_ref, s_ref = refs[18:21]
    c = pl.program_id(1)
    is_meta = c == 0

    @pl.when(c == 0)
    def _():
        s_ref[...] = jnp.zeros_like(s_ref)

    pad = jnp.zeros((CHUNK - N_META, RWKV_WIDTH), F32)
    row = lax.broadcasted_iota(jnp.int32, (CHUNK, CHUNK), 0)
    colc = lax.broadcasted_iota(jnp.int32, (CHUNK, CHUNK), 1)
    m_incl = {True: (colc <= row).astype(BF16), False: (colc >= row).astype(BF16)}

    def load(refs6, j, fwd):
        vals = []
        for idx, ref in enumerate(refs6):
            x = ref[j] if idx < 3 else ref[0, j]
            if fwd:
                mref = meta_refs[idx]
                meta = mref[j] if idx < 3 else mref[0, j]
                x = jnp.where(is_meta, jnp.concatenate([meta, pad], axis=0), x)
            vals.append(x)
        return vals

    streams = [(load(fwd_refs, j, True), True) for j in range(bb)]
    streams += [(load(bwd_refs, j, False), False) for j in range(bb)]

    prep = []
    for (r, v, kk, lw, kd, b), fwd in streams:
        lw_hi = lw.astype(BF16)
        lw_lo = (lw - lw_hi.astype(F32)).astype(BF16)
        ci = (jnp.dot(m_incl[fwd], lw_hi, preferred_element_type=F32)
              + jnp.dot(m_incl[fwd], lw_lo, preferred_element_type=F32))
        tot = jnp.sum(lw, axis=0, keepdims=True)
        e_inv = jnp.exp(-ci)
        e_rest = jnp.exp(tot - ci)
        prep.append(dict(fwd=fwd, gam=jnp.exp(tot), rt=r * jnp.exp(ci), at=-kk * jnp.exp(ci - lw), kt=kd * e_inv,
                         bt=b * e_inv, kh=kd * e_rest, bh=b * e_rest, v=v))

    group_w = HEAD_DIM * SCAN_GROUP
    n_groups = RWKV_HEADS // SCAN_GROUP
    rowg = lax.broadcasted_iota(jnp.int32, (CHUNK, group_w), 0)
    laneg = lax.broadcasted_iota(jnp.int32, (CHUNK, group_w), 1)
    srcg = laneg % CHUNK
    head_of = laneg // HEAD_DIM
    hmask = [head_of == i for i in range(SCAN_GROUP)]
    eye = (srcg == rowg).astype(F32)
    strict = {True: srcg < rowg, False: srcg > rowg}
    incl = {True: srcg <= rowg, False: srcg >= rowg}

    def bd(x):
        return jnp.concatenate([jnp.where(m, x, 0.0) for m in hmask], axis=0).astype(BF16)

    chains = [(si, gq) for si in range(len(prep)) for gq in range(n_groups)]

    def grp(name, si, gq):
        return prep[si][name][:, group_w * gq:group_w * (gq + 1)]

    def stage(fn):
        return {ch: fn(*ch) for ch in chains}

    def mm(a, b):
        return jnp.dot(a.astype(BF16), b, preferred_element_type=F32)

    contract = (((1,), (1,)), ((), ()))
    gmat = stage(lambda si, gq: lax.dot_general(
        jnp.concatenate([grp('at', si, gq), grp('rt', si, gq)], axis=0).astype(BF16),
        jnp.concatenate([bd(grp('bt', si, gq)), bd(grp('kt', si, gq))], axis=0), contract,
        preferred_element_type=F32))
    fwd_of = {ch: prep[ch[0]]['fwd'] for ch in chains}
    pmat = stage(lambda si, gq: jnp.where(strict[fwd_of[si, gq]], gmat[si, gq][:CHUNK, :group_w], 0.0))
    bdv = stage(lambda si, gq: bd(grp('v', si, gq)))
    akv = stage(lambda si, gq: mm(jnp.where(strict[fwd_of[si, gq]], gmat[si, gq][:CHUNK, group_w:], 0.0),
                                  bdv[si, gq]))
    tmat = stage(lambda si, gq: eye + pmat[si, gq])
    pmat = stage(lambda si, gq: mm(pmat[si, gq], bd(pmat[si, gq])))
    for _ in range(4):
        tmat = stage(lambda si, gq: tmat[si, gq] + mm(pmat[si, gq], bd(tmat[si, gq])))
        pmat = stage(lambda si, gq: mm(pmat[si, gq], bd(pmat[si, gq])))
    tmat = stage(lambda si, gq: tmat[si, gq] + mm(pmat[si, gq], bd(tmat[si, gq])))
    wmat = stage(lambda si, gq: mm(tmat[si, gq], bd(grp('at', si, gq))))
    uvmat = stage(lambda si, gq: mm(tmat[si, gq], bd(akv[si, gq])))
    arb = stage(lambda si, gq: jnp.where(incl[fwd_of[si, gq]], gmat[si, gq][CHUNK:, :group_w], 0.0).astype(BF16))
    rp = stage(lambda si, gq: grp('rt', si, gq) + jnp.dot(arb[si, gq], bd(wmat[si, gq]),
                                                          preferred_element_type=F32))
    yp = stage(lambda si, gq: jnp.dot(arb[si, gq], bd(uvmat[si, gq]), preferred_element_type=F32)
               + mm(jnp.where(incl[fwd_of[si, gq]], gmat[si, gq][CHUNK:, group_w:], 0.0), bdv[si, gq]))

    pairs = [(si, pr) for si in range(len(prep)) for pr in range(RWKV_HEADS // 2)]
    row2 = lax.broadcasted_iota(jnp.int32, (LANES, LANES), 0)
    lane2 = lax.broadcasted_iota(jnp.int32, (LANES, LANES), 1)
    same_head = (row2 // HEAD_DIM) == (lane2 // HEAD_DIM)
    diag2 = row2 == lane2
    zeros_pair = jnp.zeros((CHUNK, LANES), F32)

    def pair_of(x, si, pr):
        gq, off = divmod(pr * LANES, group_w)
        return x[si, gq][:, off:off + LANES]

    def pslice(name, si, pr):
        return prep[si][name][:, LANES * pr:LANES * (pr + 1)]

    def out_bot(si, pr):
        bk_t = jnp.concatenate([pslice('bh', si, pr), pslice('kh', si, pr)], axis=0).T
        rhs = jnp.concatenate(
            [jnp.concatenate([pair_of(wmat, si, pr), pair_of(uvmat, si, pr)], axis=1),
             jnp.concatenate([zeros_pair, pslice('v', si, pr)], axis=1)], axis=0).astype(BF16)
        return mm(bk_t, rhs)

    ob = {pp: out_bot(*pp) for pp in pairs}
    new_state = {}
    y_out = {}
    for si, pr in pairs:
        m_pair = jnp.where(same_head, ob[si, pr][:, :LANES], 0.0) + jnp.where(diag2, pslice('gam', si, pr), 0.0)
        n_pair = jnp.where(same_head, ob[si, pr][:, LANES:], 0.0)
        s_b = s_ref[si, pr].astype(BF16)
        y_out[si, pr] = jnp.dot(pair_of(rp, si, pr).astype(BF16), s_b, preferred_element_type=F32) \
            + pair_of(yp, si, pr)
        new_state[si, pr] = jnp.dot(m_pair.astype(BF16), s_b, preferred_element_type=F32) + n_pair

    for j in range(bb):
        for pr in range(RWKV_HEADS // 2):
            s_ref[j, pr] = new_state[j, pr]
            yf_ref[j, :, LANES * pr:LANES * (pr + 1)] = y_out[j, pr]

    @pl.when(c < ncr)
    def _():
        for j in range(bb):
            for pr in range(RWKV_HEADS // 2):
                s_ref[bb + j, pr] = new_state[bb + j, pr]
                yb_ref[j, :, LANES * pr:LANES * (pr + 1)] = y_out[bb + j, pr]


def _scan(real, meta, batch, seq, bb):
    ncr = seq // CHUNK
    assert batch % bb == 0

    def fidx(c):
        return jnp.maximum(c - 1, 0)

    def bidx(c):
        return jnp.maximum(ncr - 1 - c, 0)

    def specs(idx, d):
        one = pl.BlockSpec((bb, CHUNK, RWKV_WIDTH), lambda g, c: (g, idx(c), 0))
        two = pl.BlockSpec((1, bb, CHUNK, RWKV_WIDTH), lambda g, c: (d, g, idx(c), 0))
        return [one, one, one, two, two, two]

    mone = pl.BlockSpec((bb, N_META, RWKV_WIDTH), lambda g, c: (g, 0, 0))
    mtwo = pl.BlockSpec((1, bb, N_META, RWKV_WIDTH), lambda g, c: (0, g, 0, 0))
    y_shape = jax.ShapeDtypeStruct((batch, seq, RWKV_WIDTH), F32)
    return pl.pallas_call(
        functools.partial(_scan_kernel, ncr, bb),
        grid=(batch // bb, ncr + 1),
        in_specs=specs(fidx, 0) + specs(bidx, 1) + [mone, mone, mone, mtwo, mtwo, mtwo],
        out_specs=(pl.BlockSpec((bb, CHUNK, RWKV_WIDTH), lambda g, c: (g, fidx(c), 0)),
                   pl.BlockSpec((bb, CHUNK, RWKV_WIDTH), lambda g, c: (g, bidx(c), 0))),
        out_shape=(y_shape, y_shape),
        scratch_shapes=[pltpu.VMEM((2 * bb, RWKV_HEADS // 2, LANES, LANES), F32)],
        compiler_params=pltpu.CompilerParams(
            dimension_semantics=("parallel", "arbitrary"), vmem_limit_bytes=VMEM_LIMIT),
        name="wkv_scan",
    )(*real, *real, *meta)


def _attn_kernel(nkb, tk, group, q_ref, k_ref, vt_ref, km_ref, vmt_ref, lam_ref, sg_ref, o_ref):
    q = q_ref[0]
    tq = q.shape[0]
    lane = lax.broadcasted_iota(jnp.int32, (tq, LANES), 1)
    zero = jnp.zeros_like(q)
    qc = (jnp.where(lane < DIFF_QK_DIM, q, zero), jnp.where(lane >= DIFF_QK_DIM, q, zero))
    contract = (((1,), (1,)), ((), ()))

    def blocks(kbs, vtbs, state):
        scores = [[lax.dot_general(kb, qc[comp], contract, preferred_element_type=F32) for comp in range(2)]
                  for kb in kbs]
        state = list(state)
        for s_pair, vtb in zip(scores, vtbs):
            for comp in range(2):
                m, acc = state[comp]
                s = s_pair[comp]
                m_new = jnp.maximum(m, jnp.max(s, axis=0, keepdims=True))
                pexp = jnp.exp2(s - m_new).astype(BF16)
                acc_new = jnp.exp2(m - m_new) * acc + jnp.dot(vtb, pexp, preferred_element_type=F32)
                state[comp] = (m_new, acc_new)
        return tuple(state)

    def group_at(i):
        starts = [pl.multiple_of((i * group + j) * tk, tk) for j in range(group)]
        return ([k_ref[0, pl.ds(st_j, tk), :] for st_j in starts],
                [vt_ref[0, :, pl.ds(st_j, tk)] for st_j in starts])

    init = tuple((jnp.full((1, tq), -jnp.inf, F32), jnp.zeros((VT_ROWS, tq), F32)) for _ in range(2))
    kbs0, vtbs0 = group_at(0)
    state = blocks([km_ref[0]] + kbs0, [vmt_ref[0]] + vtbs0, init)
    state = lax.fori_loop(1, nkb // group, lambda i, st: blocks(*group_at(i), st), state)
    lam_v = lam_ref[...]
    lam = (jnp.exp(jnp.sum(lam_v[0:1] * lam_v[1:2], axis=-1, keepdims=True))
           - jnp.exp(jnp.sum(lam_v[2:3] * lam_v[3:4], axis=-1, keepdims=True)) + LAMBDA_INIT)
    (_, acc0), (_, acc1) = state
    a0, l0 = acc0[:DIFF_V_DIM], acc0[DIFF_V_DIM:DIFF_V_DIM + 1]
    a1, l1 = acc1[:DIFF_V_DIM], acc1[DIFF_V_DIM:DIFF_V_DIM + 1]
    o = a0 / l0 - lam * (a1 / l1)
    o = o * lax.rsqrt(jnp.mean(o * o, axis=0, keepdims=True) + NORM_EPS) * sg_ref[...]
    o_ref[0] = (o * (1.0 - LAMBDA_INIT)).T


def _attention(q, k, vt, km, vmt, lam, subln, tq, tk):
    batch, seq, _ = q.shape
    nkb = seq // tk
    return pl.pallas_call(
        functools.partial(_attn_kernel, nkb, tk, math.gcd(nkb, ATTN_GROUP)),
        grid=(batch, DIFF_HEADS, seq // tq),
        in_specs=[
            pl.BlockSpec((1, tq, LANES), lambda b, h, i: (b, i, h)),
            pl.BlockSpec((1, seq, LANES), lambda b, h, i: (b, 0, h)),
            pl.BlockSpec((1, VT_ROWS, seq), lambda b, h, i: (b, h, 0)),
            pl.BlockSpec((1, N_META, LANES), lambda b, h, i: (b, 0, h)),
            pl.BlockSpec((1, VT_ROWS, N_META), lambda b, h, i: (b, h, 0)),
            _const_spec((4, DIFF_QK_DIM)),
            _const_spec((DIFF_V_DIM, 1)),
        ],
        out_specs=pl.BlockSpec((1, tq, LANES), lambda b, h, i: (b, i, h)),
        out_shape=jax.ShapeDtypeStruct((batch, seq, DIFF_WIDTH), F32),
        compiler_params=pltpu.CompilerParams(
            dimension_semantics=("parallel", "parallel", "parallel"), vmem_limit_bytes=VMEM_LIMIT),
        name="diff_attn",
    )(q, k, vt, km, vmt, lam, subln)


def _post_kernel(x_ref, yf_ref, ybw_ref, bonus_ref, g_ref, yb_ref, lng_ref, lnb_ref, seg_ref, wout_ref, ln2_ref,
                 wg_ref, wu_ref, wd_ref, o_ref):
    ys = yf_ref[...] + ybw_ref[...]
    inv_n = 1.0 / HEAD_DIM
    mu = _seg_sum(ys, seg_ref) * inv_n
    dev = ys - mu
    var = _seg_sum(dev * dev, seg_ref) * inv_n
    yn = dev * lax.rsqrt(var + GN_EPS) * lng_ref[...] + lnb_ref[...]
    ya = (yn + bonus_ref[...]) * g_ref[...]
    mix = jnp.concatenate([ya, yb_ref[...]], axis=1).astype(BF16)
    x1 = x_ref[...] + jnp.dot(mix, wout_ref[...], preferred_element_type=F32)
    n2 = x1 * lax.rsqrt(jnp.mean(x1 * x1, axis=-1, keepdims=True) + NORM_EPS) * ln2_ref[...]
    n2 = n2.astype(BF16)
    gate = jnp.dot(n2, wg_ref[...], preferred_element_type=F32)
    up = jnp.dot(n2, wu_ref[...], preferred_element_type=F32)
    hid = (gate * jax.nn.sigmoid(gate) * up).astype(BF16)
    o_ref[...] = x1 + jnp.dot(hid, wd_ref[...], preferred_element_type=F32)


def _post(x2, y_f, y_b, bonus, g, yb, weights, tt):
    rows = x2.shape[0]
    row512 = pl.BlockSpec((tt, RWKV_WIDTH), lambda i: (i, 0))
    in_specs = [
        pl.BlockSpec((tt, D_MODEL), lambda i: (i, 0)),
        row512, row512, row512, row512, row512,
        _const_spec((1, RWKV_WIDTH)), _const_spec((1, RWKV_WIDTH)), _const_spec((256, 256)),
        _const_spec((D_MODEL, D_MODEL)), _const_spec((1, D_MODEL)),
        _const_spec((D_MODEL, D_FF)), _const_spec((D_MODEL, D_FF)), _const_spec((D_FF, D_MODEL)),
    ]
    return pl.pallas_call(
        _post_kernel,
        grid=(rows // tt,),
        in_specs=in_specs,
        out_specs=pl.BlockSpec((tt, D_MODEL), lambda i: (i, 0)),
        out_shape=jax.ShapeDtypeStruct((rows, D_MODEL), F32),
        compiler_params=pltpu.CompilerParams(dimension_semantics=("parallel",), vmem_limit_bytes=VMEM_LIMIT),
        name="post",
    )(x2, y_f, y_b, bonus, g, yb, *weights)


def _rope_tables(total):
    freqs = ROPE_THETA ** (-jnp.arange(0, ROPE_DIMS, 2, dtype=F32) / ROPE_DIMS)
    ang = jnp.arange(total, dtype=F32)[:, None] * freqs[None, :]
    cos, sin = jnp.cos(ang), jnp.sin(ang)
    ones = jnp.ones((total, DIFF_QK_DIM - ROPE_DIMS), F32)
    zeros = jnp.zeros((total, ROPE_HALF), F32)
    zrest = jnp.zeros((total, DIFF_QK_DIM - ROPE_DIMS), F32)
    cos64 = jnp.concatenate([cos, cos, ones], axis=1)
    sa64 = jnp.concatenate([-sin, zeros, zrest], axis=1)
    sb64 = jnp.concatenate([zeros, sin, zrest], axis=1)
    return tuple(jnp.concatenate([t, t], axis=1) for t in (cos64, sa64, sb64))


def _prepare_weights(p):
    def lora_cols(scale_fn):
        cols = [scale_fn(p['mix_w'][0])[:, None] * p['decay_w1'][0],
                scale_fn(p['mix_w'][1])[:, None] * p['decay_w1'][1],
                scale_fn(p['mix_a'][0])[:, None] * p['aaa_a1'][0],
                scale_fn(p['mix_a'][1])[:, None] * p['aaa_a1'][1],
                scale_fn(p['mix_g'])[:, None] * p['gate_g1'],
                jnp.zeros((D_MODEL, GATE_LORA_PAD - GATE_LORA), F32)]
        return jnp.concatenate(cols, axis=1)

    wcat = jnp.concatenate([p['w_in'], lora_cols(lambda m: 1.0 - m), lora_cols(lambda m: 0.5 * m)], axis=1)

    def blockdiag(w):
        z = jnp.zeros_like(w[0])
        return jnp.concatenate([jnp.concatenate([w[0], z], axis=1), jnp.concatenate([z, w[1]], axis=1)], axis=0)

    g2 = jnp.concatenate([p['gate_g2'], jnp.zeros((GATE_LORA_PAD - GATE_LORA, RWKV_WIDTH), F32)], axis=0)
    seg = (jnp.arange(256)[:, None] // HEAD_DIM == jnp.arange(256)[None, :] // HEAD_DIM).astype(BF16)
    pre_w = (
        p['ln1_g'][None, :], wcat.astype(BF16), p['conv_rkv'],
        p['decay_w0'].reshape(1, 2 * RWKV_WIDTH), blockdiag(p['decay_w2']).astype(BF16),
        p['aaa_a0'].reshape(1, 2 * RWKV_WIDTH), blockdiag(p['aaa_a2']).astype(BF16),
        g2.astype(BF16), p['k_k'][None, :], p['k_a'][None, :], p['r_k'].reshape(1, RWKV_WIDTH),
        jnp.tile(p['q_norm_g'], DIFF_WIDTH // DIFF_QK_DIM)[None, :],
        jnp.tile(p['k_norm_g'], DIFF_WIDTH // DIFF_QK_DIM)[None, :], seg,
    )
    post_w = (
        p['lnx_g'][None, :], p['lnx_b'][None, :], seg, p['w_out'].astype(BF16), p['ln2_g'][None, :],
        p['w_gate'].astype(BF16), p['w_up'].astype(BF16), p['w_down'].astype(BF16),
    )
    return pre_w, post_w


def _tile_rows(seq, cap):
    tt = min(seq, cap)
    assert seq % tt == 0 and tt % CHUNK == 0
    return tt


def _trunk(x, meta, pre_w, post_w, lam, subln, tt_pre, tt_post, tq, tk, scan_bb):
    batch, seq, _ = x.shape
    rows = batch * seq
    x2 = x.reshape(rows, D_MODEL)
    tabs = _rope_tables(N_META + seq)
    tabs_meta = tuple(t[:N_META] for t in tabs)
    tabs_real = tuple(t[N_META:] for t in tabs)

    real = _pre_real(x2, meta, tabs_real, pre_w, seq, _tile_rows(seq, tt_pre))
    metao = _pre_meta(x, meta, tabs_meta, pre_w)
    (r, v, kk, lw, kd, b, g, bonus, q, k, vd) = real
    (mr, mv, mkk, mlw, mkd, mb, _, _, _, mk, mvd) = metao

    def r3(a, n):
        return a.reshape(a.shape[:-2] + (batch, n, a.shape[-1]))

    y_f, y_b = _scan((r3(r, seq), r3(v, seq), r3(kk, seq), r3(lw, seq), r3(kd, seq), r3(b, seq)),
                     (r3(mr, N_META), r3(mv, N_META), r3(mkk, N_META), r3(mlw, N_META), r3(mkd, N_META),
                      r3(mb, N_META)), batch, seq, scan_bb)
    yb = _attention(r3(q, seq), r3(k, seq), vd, r3(mk, N_META), mvd, lam, subln,
                    _tile_rows(seq, tq), _tile_rows(seq, tk))
    out = _post(x2, y_f.reshape(rows, RWKV_WIDTH), y_b.reshape(rows, RWKV_WIDTH), bonus, g,
                yb.reshape(rows, DIFF_WIDTH), post_w,
                _tile_rows(seq, tt_post))
    return out.reshape(batch, seq, D_MODEL)


def kernel(x_prompt, x_sample, meta_tokens, ln1_g, w_in, conv_rkv, mix_w, mix_a, mix_g, decay_w0, decay_w1,
           decay_w2, aaa_a0, aaa_a1, aaa_a2, gate_g1, gate_g2, k_k, k_a, r_k, lnx_g, lnx_b, q_norm_g, k_norm_g,
           diff_lambdas, subln_g, w_out, ln2_g, w_gate, w_up, w_down):
    params = dict(ln1_g=ln1_g, w_in=w_in, conv_rkv=conv_rkv, mix_w=mix_w, mix_a=mix_a, mix_g=mix_g,
                  decay_w0=decay_w0, decay_w1=decay_w1, decay_w2=decay_w2, aaa_a0=aaa_a0, aaa_a1=aaa_a1,
                  aaa_a2=aaa_a2, gate_g1=gate_g1, gate_g2=gate_g2, k_k=k_k, k_a=k_a, r_k=r_k, lnx_g=lnx_g,
                  lnx_b=lnx_b, q_norm_g=q_norm_g, k_norm_g=k_norm_g, w_out=w_out, ln2_g=ln2_g, w_gate=w_gate,
                  w_up=w_up, w_down=w_down)
    p = {name: arr[0] for name, arr in params.items()}
    pre_w, post_w = _prepare_weights(p)
    lam = diff_lambdas[0]
    subln = subln_g[0][:, None]
    outs = []
    for x in (x_prompt, x_sample):
        outs.append(_trunk(x, meta_tokens, pre_w, post_w, lam, subln, 256, 256, 512, 256, 2))
    return tuple(outs)
```

```python
import functools
import math

import jax
import jax.numpy as jnp
from jax import lax
from jax.experimental import pallas as pl
from jax.experimental.pallas import tpu as pltpu

F32 = jnp.float32
BF16 = jnp.bfloat16

D_MODEL = 1024
N_META = 16
RWKV_HEADS = 8
HEAD_DIM = 64
RWKV_WIDTH = 512
DIFF_HEADS = 4
DIFF_QK_DIM = 64
DIFF_V_DIM = 128
DIFF_WIDTH = 512
IN_WIDTH = 3072
DECAY_LORA = 64
AAA_LORA = 64
GATE_LORA = 160
GATE_LORA_PAD = 256
LORA_BLOCK = 512
CAT_WIDTH = IN_WIDTH + 2 * LORA_BLOCK
ROPE_THETA = 500000.0
ROPE_DIMS = 16
ROPE_HALF = 8
D_FF = 2816
NORM_EPS = 1e-6
GN_EPS = 64e-5
DECAY_SCALE = math.exp(-0.5)
LAMBDA_INIT = 0.8 - 0.6 * math.exp(0.0)

LANES = 128
SUBLANES = 8
HALO = SUBLANES
CHUNK = 64
ATTN_GROUP = 8
VT_ROWS = DIFF_V_DIM + 16
LOG2E = math.log2(math.e)
VMEM_LIMIT = 56 * 1024 * 1024


def _const_spec(shape):
    zeros = (0,) * len(shape)
    return pl.BlockSpec(shape, lambda *_: zeros)


def _seg_sum(x, seg_ref):
    seg = seg_ref[...]
    parts = []
    for c in range(x.shape[1] // 256):
        xs = x[:, 256 * c:256 * (c + 1)].astype(BF16)
        parts.append(jnp.dot(xs, seg, preferred_element_type=F32))
    return jnp.concatenate(parts, axis=1)


def _roll_lanes(x, shift):
    parts = [pltpu.roll(x[:, LANES * c:LANES * (c + 1)], shift, 1) for c in range(x.shape[1] // LANES)]
    return jnp.concatenate(parts, axis=1)


def _pre_compute(xx, tt, tabs, w, outs):
    (cos_ref, sa_ref, sb_ref) = tabs
    (ln1_ref, wcat_ref, conv_ref, w0_ref, w2_ref, a0_ref, a2_ref, g2_ref, kk_ref, ka_ref, rk_ref,
     qg_ref, kg_ref, seg_ref) = w
    (o_r, o_v, o_kk, o_lw, o_kd, o_b, o_g, o_bonus, o_q, o_k, o_vd) = outs
    rows = tt + 2 * HALO

    ms = jnp.mean(xx * xx, axis=-1, keepdims=True)
    n = xx * lax.rsqrt(ms + NORM_EPS) * ln1_ref[...]
    p = jnp.dot(n.astype(BF16), wcat_ref[...], preferred_element_type=F32)

    def prev_rows(a):
        return pltpu.roll(a, 1, 0)[HALO:HALO + tt]

    def next_rows(a):
        return pltpu.roll(a, rows - 1, 0)[HALO:HALO + tt]

    c0 = 3 * RWKV_WIDTH
    rkv_in = p[:, :c0]
    conv = conv_ref[...]
    rkv = conv[0:1] * prev_rows(rkv_in) + conv[1:2] * rkv_in[HALO:HALO + tt] + conv[2:3] * next_rows(rkv_in)
    r = rkv[:, :RWKV_WIDTH]
    k = rkv[:, RWKV_WIDTH:2 * RWKV_WIDTH]
    v = rkv[:, 2 * RWKV_WIDTH:]

    l_self = p[HALO:HALO + tt, IN_WIDTH:IN_WIDTH + LORA_BLOCK]
    l_nbr = p[:, IN_WIDTH + LORA_BLOCK:]
    lora = l_self + prev_rows(l_nbr) + next_rows(l_nbr)

    tw = jnp.tanh(lora[:, 0:128]).astype(BF16)
    zw = jnp.dot(tw, w2_ref[...], preferred_element_type=F32) + w0_ref[...]
    lw = -DECAY_SCALE * jax.nn.sigmoid(zw)
    za = jnp.dot(lora[:, 128:256].astype(BF16), a2_ref[...], preferred_element_type=F32) + a0_ref[...]
    aa = jax.nn.sigmoid(za)
    sg = jax.nn.sigmoid(lora[:, 256:512]).astype(BF16)
    g = jnp.dot(sg, g2_ref[...], preferred_element_type=F32)

    kk = k * kk_ref[...]
    kk = kk * lax.rsqrt(jnp.maximum(_seg_sum(kk * kk, seg_ref), 1e-24))
    k_a = ka_ref[...]
    kd0 = k * (1.0 + (aa[:, :RWKV_WIDTH] - 1.0) * k_a)
    kd1 = k * (1.0 + (aa[:, RWKV_WIDTH:] - 1.0) * k_a)
    bonus = _seg_sum(r * rk_ref[...] * (kd0 + kd1), seg_ref) * v

    o_r[...] = r
    o_v[...] = v
    o_kk[...] = kk
    o_lw[0] = lw[:, :RWKV_WIDTH]
    o_lw[1] = lw[:, RWKV_WIDTH:]
    o_kd[0] = kd0
    o_kd[1] = kd1
    o_b[0] = kk * aa[:, :RWKV_WIDTH]
    o_b[1] = kk * aa[:, RWKV_WIDTH:]
    o_g[...] = g
    o_bonus[...] = bonus

    cos_t = jnp.concatenate([cos_ref[...]] * 4, axis=1)
    sa_t = jnp.concatenate([sa_ref[...]] * 4, axis=1)
    sb_t = jnp.concatenate([sb_ref[...]] * 4, axis=1)

    def qk_norm_rope(xq, g_ref):
        ssq = _seg_sum(xq * xq, seg_ref) * (1.0 / DIFF_QK_DIM)
        xn = xq * lax.rsqrt(ssq + NORM_EPS) * g_ref[...]
        return xn * cos_t + _roll_lanes(xn, LANES - ROPE_HALF) * sa_t + _roll_lanes(xn, ROPE_HALF) * sb_t

    qd = p[HALO:HALO + tt, c0:c0 + DIFF_WIDTH]
    kd = p[HALO:HALO + tt, c0 + DIFF_WIDTH:c0 + 2 * DIFF_WIDTH]
    vd = p[HALO:HALO + tt, c0 + 2 * DIFF_WIDTH:c0 + 3 * DIFF_WIDTH]
    o_q[...] = (qk_norm_rope(qd, qg_ref) * (DIFF_QK_DIM ** -0.5 * LOG2E)).astype(BF16)
    o_k[...] = qk_norm_rope(kd, kg_ref).astype(BF16)
    vd_t = vd.T
    fill = (lax.broadcasted_iota(jnp.int32, (VT_ROWS - DIFF_V_DIM, tt), 0) == 0).astype(F32)
    o_vd[0] = jnp.concatenate(
        [piece for h in range(DIFF_HEADS) for piece in (vd_t[DIFF_V_DIM * h:DIFF_V_DIM * (h + 1)], fill)],
        axis=0).astype(BF16)


def _pre_real_kernel(nt, tt, xm_ref, xp_ref, xn_ref, meta_ref, cos_ref, sa_ref, sb_ref, *rest):
    w, outs = rest[:14], rest[14:]
    j = pl.program_id(0) % nt
    xp = jnp.where(j == 0, meta_ref[N_META - HALO:N_META, :], xp_ref[...])
    xn = jnp.where(j == nt - 1, 0.0, xn_ref[...])
    xx = jnp.concatenate([xp, xm_ref[...], xn], axis=0)
    _pre_compute(xx, tt, (cos_ref, sa_ref, sb_ref), w, outs)


def _pre_meta_kernel(xn_ref, meta_ref, cos_ref, sa_ref, sb_ref, *rest):
    w, outs = rest[:14], rest[14:]
    xx = jnp.concatenate([jnp.zeros((HALO, D_MODEL), F32), meta_ref[...], xn_ref[0]], axis=0)
    _pre_compute(xx, N_META, (cos_ref, sa_ref, sb_ref), w, outs)


def _pre_out_shapes(batch, seq):
    rows = batch * seq
    one = jax.ShapeDtypeStruct((rows, RWKV_WIDTH), F32)
    two = jax.ShapeDtypeStruct((2, rows, RWKV_WIDTH), F32)
    half = jax.ShapeDtypeStruct((rows, DIFF_WIDTH), BF16)
    v_t = jax.ShapeDtypeStruct((batch, DIFF_HEADS * VT_ROWS, seq), BF16)
    return (one, one, one, two, two, two, one, one, half, half, v_t)


def _pre_out_specs(tt, nt):
    one = pl.BlockSpec((tt, RWKV_WIDTH), lambda i: (i, 0))
    two = pl.BlockSpec((2, tt, RWKV_WIDTH), lambda i: (0, i, 0))
    v_t = pl.BlockSpec((1, DIFF_HEADS * VT_ROWS, tt), lambda i: (i // nt, 0, i % nt))
    return (one, one, one, two, two, two, one, one, one, one, v_t)


def _pre_weight_specs():
    shapes = [(1, D_MODEL), (D_MODEL, CAT_WIDTH), (3, 3 * RWKV_WIDTH), (1, 2 * RWKV_WIDTH),
              (2 * DECAY_LORA, 2 * RWKV_WIDTH), (1, 2 * RWKV_WIDTH), (2 * AAA_LORA, 2 * RWKV_WIDTH),
              (GATE_LORA_PAD, RWKV_WIDTH), (1, RWKV_WIDTH), (1, RWKV_WIDTH), (1, RWKV_WIDTH),
              (1, DIFF_WIDTH), (1, DIFF_WIDTH), (256, 256)]
    return [_const_spec(s) for s in shapes]


def _pre_real(x2, meta, tabs_real, weights, seq, tt):
    rows = x2.shape[0]
    nt = seq // tt
    hb = tt // HALO
    last_halo = rows // HALO - 1
    in_specs = [
        pl.BlockSpec((tt, D_MODEL), lambda i: (i, 0)),
        pl.BlockSpec((HALO, D_MODEL), lambda i: (jnp.maximum(i * hb - 1, 0), 0)),
        pl.BlockSpec((HALO, D_MODEL), lambda i: (jnp.minimum((i + 1) * hb, last_halo), 0)),
        _const_spec((N_META, D_MODEL)),
    ] + [pl.BlockSpec((tt, LANES), lambda i: (i % nt, 0))] * 3 + _pre_weight_specs()
    return pl.pallas_call(
        functools.partial(_pre_real_kernel, nt, tt),
        grid=(rows // tt,),
        in_specs=in_specs,
        out_specs=_pre_out_specs(tt, nt),
        out_shape=_pre_out_shapes(rows // seq, seq),
        compiler_params=pltpu.CompilerParams(dimension_semantics=("parallel",), vmem_limit_bytes=VMEM_LIMIT),
        name="pre_real",
    )(x2, x2, x2, meta, *tabs_real, *weights)


def _pre_meta(x3, meta, tabs_meta, weights):
    batch = x3.shape[0]
    in_specs = [
        pl.BlockSpec((1, HALO, D_MODEL), lambda b: (b, 0, 0)),
        _const_spec((N_META, D_MODEL)),
    ] + [_const_spec((N_META, LANES))] * 3 + _pre_weight_specs()
    return pl.pallas_call(
        _pre_meta_kernel,
        grid=(batch,),
        in_specs=in_specs,
        out_specs=_pre_out_specs(N_META, 1),
        out_shape=_pre_out_shapes(batch, N_META),
        compiler_params=pltpu.CompilerParams(dimension_semantics=("parallel",), vmem_limit_bytes=VMEM_LIMIT),
        name="pre_meta",
    )(x3, meta, *tabs_meta, *weights)


def _scan_kernel(ncr, bb, *refs):
    fwd_refs, bwd_refs, meta_refs = refs[0:6], refs[6:12], refs[12:18]
    yf_ref, yb_ref, s_ref = refs[18:21]
    c = pl.program_id(1)
    is_meta = c == 0

    @pl.when(c == 0)
    def _():
        s_ref[...] = jnp.zeros_like(s_ref)

    pad = jnp.zeros((CHUNK - N_META, RWKV_WIDTH), F32)
    row = lax.broadcasted_iota(jnp.int32, (CHUNK, CHUNK), 0)
    colc = lax.broadcasted_iota(jnp.int32, (CHUNK, CHUNK), 1)
    m_incl = {True: (colc <= row).astype(BF16), False: (colc >= row).astype(BF16)}

    def load(refs6, j, fwd):
        vals = []
        for idx, ref in enumerate(refs6):
            x = ref[j] if idx < 3 else ref[0, j]
            if fwd:
                mref = meta_refs[idx]
                meta = mref[j] if idx < 3 else mref[0, j]
                x = jnp.where(is_meta, jnp.concatenate([meta, pad], axis=0), x)
            vals.append(x)
        return vals

    streams = [(load(fwd_refs, j, True), True) for j in range(bb)]
    streams += [(load(bwd_refs, j, False), False) for j in range(bb)]

    prep = []
    for (r, v, kk, lw, kd, b), fwd in streams:
        lw_hi = lw.astype(BF16)
        lw_lo = (lw - lw_hi.astype(F32)).astype(BF16)
        ci = (jnp.dot(m_incl[fwd], lw_hi, preferred_element_type=F32)
              + jnp.dot(m_incl[fwd], lw_lo, preferred_element_type=F32))
        tot = jnp.sum(lw, axis=0, keepdims=True)
        e_inv = jnp.exp(-ci)
        e_rest = jnp.exp(tot - ci)
        prep.append(dict(fwd=fwd, gam=jnp.exp(tot), rt=r * jnp.exp(ci), at=-kk * jnp.exp(ci - lw), kt=kd * e_inv,
                         bt=b * e_inv, kh=kd * e_rest, bh=b * e_rest, v=v))

    rowp = lax.broadcasted_iota(jnp.int32, (CHUNK, LANES), 0)
    lanep = lax.broadcasted_iota(jnp.int32, (CHUNK, LANES), 1)
    left = lanep < HEAD_DIM
    eye = (lanep % CHUNK == rowp).astype(F32)
    row2 = lax.broadcasted_iota(jnp.int32, (CHUNK, 2 * LANES), 0)
    src2 = lax.broadcasted_iota(jnp.int32, (CHUNK, 2 * LANES), 1) % CHUNK
    strict = {True: src2 < row2, False: src2 > row2}
    incl = {True: src2 <= row2, False: src2 >= row2}

    def bd(x):
        return jnp.concatenate([jnp.where(left, x, 0.0), jnp.where(left, 0.0, x)], axis=0).astype(BF16)

    chains = [(si, pr) for si in range(len(prep)) for pr in range(RWKV_HEADS // 2)]

    def pslice(name, si, pr):
        return prep[si][name][:, LANES * pr:LANES * (pr + 1)]

    def stage(fn):
        return {ch: fn(*ch) for ch in chains}

    def mm(a, b):
        return jnp.dot(a.astype(BF16), b, preferred_element_type=F32)

    fwd_of = {ch: prep[ch[0]]['fwd'] for ch in chains}
    contract = (((1,), (1,)), ((), ()))
    gmat = stage(lambda si, pr: lax.dot_general(
        jnp.concatenate([pslice('at', si, pr), pslice('rt', si, pr)], axis=0).astype(BF16),
        jnp.concatenate([bd(pslice('bt', si, pr)), bd(pslice('kt', si, pr))], axis=0), contract,
        preferred_element_type=F32))
    g_strict = stage(lambda si, pr: jnp.where(strict[fwd_of[si, pr]], gmat[si, pr][:CHUNK], 0.0))
    bdv = stage(lambda si, pr: bd(pslice('v', si, pr)))
    akv = stage(lambda si, pr: mm(g_strict[si, pr][:, LANES:], bdv[si, pr]))
    tmat = stage(lambda si, pr: eye + g_strict[si, pr][:, :LANES])
    pmat = stage(lambda si, pr: mm(g_strict[si, pr][:, :LANES], bd(g_strict[si, pr][:, :LANES])))
    for _ in range(4):
        res = stage(lambda si, pr: mm(pmat[si, pr], jnp.concatenate([bd(tmat[si, pr]), bd(pmat[si, pr])], axis=1)))
        tmat = stage(lambda si, pr: tmat[si, pr] + res[si, pr][:, :LANES])
        pmat = stage(lambda si, pr: res[si, pr][:, LANES:])
    tmat = stage(lambda si, pr: tmat[si, pr] + mm(pmat[si, pr], bd(tmat[si, pr])))
    wu = stage(lambda si, pr: mm(tmat[si, pr], jnp.concatenate([bd(pslice('at', si, pr)), bd(akv[si, pr])], axis=1)))
    zeros_bd = jnp.zeros((LANES, LANES), BF16)

    def out_top(si, pr):
        w_bd, u_bd = bd(wu[si, pr][:, :LANES]), bd(wu[si, pr][:, LANES:])
        rhs = jnp.concatenate([jnp.concatenate([w_bd, u_bd], axis=1),
                               jnp.concatenate([zeros_bd, bdv[si, pr]], axis=1)], axis=0)
        return mm(jnp.where(incl[fwd_of[si, pr]], gmat[si, pr][CHUNK:], 0.0), rhs)

    ot = stage(out_top)

    rows_p = lax.broadcasted_iota(jnp.int32, (LANES, LANES), 0)
    lanes_p = lax.broadcasted_iota(jnp.int32, (LANES, LANES), 1)
    same_head = (rows_p // HEAD_DIM) == (lanes_p // HEAD_DIM)
    diag_p = rows_p == lanes_p
    zeros_pair = jnp.zeros((CHUNK, LANES), F32)

    def out_bot(si, pr):
        bk_t = jnp.concatenate([pslice('bh', si, pr), pslice('kh', si, pr)], axis=0).T
        rhs = jnp.concatenate([wu[si, pr], jnp.concatenate([zeros_pair, pslice('v', si, pr)], axis=1)],
                              axis=0).astype(BF16)
        return mm(bk_t, rhs)

    ob = stage(out_bot)
    new_state = {}
    y_out = {}
    for si, pr in chains:
        m_pair = jnp.where(same_head, ob[si, pr][:, :LANES], 0.0) + jnp.where(diag_p, pslice('gam', si, pr), 0.0)
        n_pair = jnp.where(same_head, ob[si, pr][:, LANES:], 0.0)
        s_b = s_ref[si, pr].astype(BF16)
        y_out[si, pr] = mm(pslice('rt', si, pr) + ot[si, pr][:, :LANES], s_b) + ot[si, pr][:, LANES:]
        new_state[si, pr] = mm(m_pair, s_b) + n_pair

    for j in range(bb):
        for pr in range(RWKV_HEADS // 2):
            s_ref[j, pr] = new_state[j, pr]
            yf_ref[j, :, LANES * pr:LANES * (pr + 1)] = y_out[j, pr]

    @pl.when(c < ncr)
    def _():
        for j in range(bb):
            for pr in range(RWKV_HEADS // 2):
                s_ref[bb + j, pr] = new_state[bb + j, pr]
                yb_ref[j, :, LANES * pr:LANES * (pr + 1)] = y_out[bb + j, pr]


def _scan(real, meta, batch, seq, bb):
    ncr = seq // CHUNK
    assert batch % bb == 0

    def fidx(c):
        return jnp.maximum(c - 1, 0)

    def bidx(c):
        return jnp.maximum(ncr - 1 - c, 0)

    def specs(idx, d):
        one = pl.BlockSpec((bb, CHUNK, RWKV_WIDTH), lambda g, c: (g, idx(c), 0))
        two = pl.BlockSpec((1, bb, CHUNK, RWKV_WIDTH), lambda g, c: (d, g, idx(c), 0))
        return [one, one, one, two, two, two]

    mone = pl.BlockSpec((bb, N_META, RWKV_WIDTH), lambda g, c: (g, 0, 0))
    mtwo = pl.BlockSpec((1, bb, N_META, RWKV_WIDTH), lambda g, c: (0, g, 0, 0))
    y_shape = jax.ShapeDtypeStruct((batch, seq, RWKV_WIDTH), F32)
    return pl.pallas_call(
        functools.partial(_scan_kernel, ncr, bb),
        grid=(batch // bb, ncr + 1),
        in_specs=specs(fidx, 0) + specs(bidx, 1) + [mone, mone, mone, mtwo, mtwo, mtwo],
        out_specs=(pl.BlockSpec((bb, CHUNK, RWKV_WIDTH), lambda g, c: (g, fidx(c), 0)),
                   pl.BlockSpec((bb, CHUNK, RWKV_WIDTH), lambda g, c: (g, bidx(c), 0))),
        out_shape=(y_shape, y_shape),
        scratch_shapes=[pltpu.VMEM((2 * bb, RWKV_HEADS // 2, LANES, LANES), F32)],
        compiler_params=pltpu.CompilerParams(
            dimension_semantics=("parallel", "arbitrary"), vmem_limit_bytes=VMEM_LIMIT),
        name="wkv_scan",
    )(*real, *real, *meta)


def _attn_kernel(nkb, tk, group, q_ref, k_ref, vt_ref, km_ref, vmt_ref, lam_ref, sg_ref, o_ref):
    q = q_ref[0]
    tq = q.shape[0]
    lane = lax.broadcasted_iota(jnp.int32, (tq, LANES), 1)
    zero = jnp.zeros_like(q)
    qc = (jnp.where(lane < DIFF_QK_DIM, q, zero), jnp.where(lane >= DIFF_QK_DIM, q, zero))
    contract = (((1,), (1,)), ((), ()))

    def blocks(kbs, vtbs, state):
        scores = [[lax.dot_general(kb, qc[comp], contract, preferred_element_type=F32) for comp in range(2)]
                  for kb in kbs]
        state = list(state)
        for s_pair, vtb in zip(scores, vtbs):
            for comp in range(2):
                m, acc = state[comp]
                s = s_pair[comp]
                m_new = jnp.maximum(m, jnp.max(s, axis=0, keepdims=True))
                pexp = jnp.exp2(s - m_new).astype(BF16)
                acc_new = jnp.exp2(m - m_new) * acc + jnp.dot(vtb, pexp, preferred_element_type=F32)
                state[comp] = (m_new, acc_new)
        return tuple(state)

    def group_at(i):
        starts = [pl.multiple_of((i * group + j) * tk, tk) for j in range(group)]
        return ([k_ref[0, pl.ds(st_j, tk), :] for st_j in starts],
                [vt_ref[0, :, pl.ds(st_j, tk)] for st_j in starts])

    init = tuple((jnp.full((1, tq), -jnp.inf, F32), jnp.zeros((VT_ROWS, tq), F32)) for _ in range(2))
    kbs0, vtbs0 = group_at(0)
    state = blocks([km_ref[0]] + kbs0, [vmt_ref[0]] + vtbs0, init)
    state = lax.fori_loop(1, nkb // group, lambda i, st: blocks(*group_at(i), st), state)
    lam_v = lam_ref[...]
    lam = (jnp.exp(jnp.sum(lam_v[0:1] * lam_v[1:2], axis=-1, keepdims=True))
           - jnp.exp(jnp.sum(lam_v[2:3] * lam_v[3:4], axis=-1, keepdims=True)) + LAMBDA_INIT)
    (_, acc0), (_, acc1) = state
    a0, l0 = acc0[:DIFF_V_DIM], acc0[DIFF_V_DIM:DIFF_V_DIM + 1]
    a1, l1 = acc1[:DIFF_V_DIM], acc1[DIFF_V_DIM:DIFF_V_DIM + 1]
    o = a0 / l0 - lam * (a1 / l1)
    o = o * lax.rsqrt(jnp.mean(o * o, axis=0, keepdims=True) + NORM_EPS) * sg_ref[...]
    o_ref[0] = (o * (1.0 - LAMBDA_INIT)).T


def _attention(q, k, vt, km, vmt, lam, subln, tq, tk):
    batch, seq, _ = q.shape
    nkb = seq // tk
    return pl.pallas_call(
        functools.partial(_attn_kernel, nkb, tk, math.gcd(nkb, ATTN_GROUP)),
        grid=(batch, DIFF_HEADS, seq // tq),
        in_specs=[
            pl.BlockSpec((1, tq, LANES), lambda b, h, i: (b, i, h)),
            pl.BlockSpec((1, seq, LANES), lambda b, h, i: (b, 0, h)),
            pl.BlockSpec((1, VT_ROWS, seq), lambda b, h, i: (b, h, 0)),
            pl.BlockSpec((1, N_META, LANES), lambda b, h, i: (b, 0, h)),
            pl.BlockSpec((1, VT_ROWS, N_META), lambda b, h, i: (b, h, 0)),
            _const_spec((4, DIFF_QK_DIM)),
            _const_spec((DIFF_V_DIM, 1)),
        ],
        out_specs=pl.BlockSpec((1, tq, LANES), lambda b, h, i: (b, i, h)),
        out_shape=jax.ShapeDtypeStruct((batch, seq, DIFF_WIDTH), F32),
        compiler_params=pltpu.CompilerParams(
            dimension_semantics=("parallel", "parallel", "parallel"), vmem_limit_bytes=VMEM_LIMIT),
        name="diff_attn",
    )(q, k, vt, km, vmt, lam, subln)


def _post_kernel(x_ref, yf_ref, ybw_ref, bonus_ref, g_ref, yb_ref, lng_ref, lnb_ref, seg_ref, wout_ref, ln2_ref,
                 wg_ref, wu_ref, wd_ref, o_ref):
    ys = yf_ref[...] + ybw_ref[...]
    inv_n = 1.0 / HEAD_DIM
    mu = _seg_sum(ys, seg_ref) * inv_n
    dev = ys - mu
    var = _seg_sum(dev * dev, seg_ref) * inv_n
    yn = dev * lax.rsqrt(var + GN_EPS) * lng_ref[...] + lnb_ref[...]
    ya = (yn + bonus_ref[...]) * g_ref[...]
    mix = jnp.concatenate([ya, yb_ref[...]], axis=1).astype(BF16)
    x1 = x_ref[...] + jnp.dot(mix, wout_ref[...], preferred_element_type=F32)
    n2 = x1 * lax.rsqrt(jnp.mean(x1 * x1, axis=-1, keepdims=True) + NORM_EPS) * ln2_ref[...]
    n2 = n2.astype(BF16)
    gate = jnp.dot(n2, wg_ref[...], preferred_element_type=F32)
    up = jnp.dot(n2, wu_ref[...], preferred_element_type=F32)
    hid = (gate * jax.nn.sigmoid(gate) * up).astype(BF16)
    o_ref[...] = x1 + jnp.dot(hid, wd_ref[...], preferred_element_type=F32)


def _post(x2, y_f, y_b, bonus, g, yb, weights, tt):
    rows = x2.shape[0]
    row512 = pl.BlockSpec((tt, RWKV_WIDTH), lambda i: (i, 0))
    in_specs = [
        pl.BlockSpec((tt, D_MODEL), lambda i: (i, 0)),
        row512, row512, row512, row512, row512,
        _const_spec((1, RWKV_WIDTH)), _const_spec((1, RWKV_WIDTH)), _const_spec((256, 256)),
        _const_spec((D_MODEL, D_MODEL)), _const_spec((1, D_MODEL)),
        _const_spec((D_MODEL, D_FF)), _const_spec((D_MODEL, D_FF)), _const_spec((D_FF, D_MODEL)),
    ]
    return pl.pallas_call(
        _post_kernel,
        grid=(rows // tt,),
        in_specs=in_specs,
        out_specs=pl.BlockSpec((tt, D_MODEL), lambda i: (i, 0)),
        out_shape=jax.ShapeDtypeStruct((rows, D_MODEL), F32),
        compiler_params=pltpu.CompilerParams(dimension_semantics=("parallel",), vmem_limit_bytes=VMEM_LIMIT),
        name="post",
    )(x2, y_f, y_b, bonus, g, yb, *weights)


def _rope_tables(total):
    freqs = ROPE_THETA ** (-jnp.arange(0, ROPE_DIMS, 2, dtype=F32) / ROPE_DIMS)
    ang = jnp.arange(total, dtype=F32)[:, None] * freqs[None, :]
    cos, sin = jnp.cos(ang), jnp.sin(ang)
    ones = jnp.ones((total, DIFF_QK_DIM - ROPE_DIMS), F32)
    zeros = jnp.zeros((total, ROPE_HALF), F32)
    zrest = jnp.zeros((total, DIFF_QK_DIM - ROPE_DIMS), F32)
    cos64 = jnp.concatenate([cos, cos, ones], axis=1)
    sa64 = jnp.concatenate([-sin, zeros, zrest], axis=1)
    sb64 = jnp.concatenate([zeros, sin, zrest], axis=1)
    return tuple(jnp.concatenate([t, t], axis=1) for t in (cos64, sa64, sb64))


def _prepare_weights(p):
    def lora_cols(scale_fn):
        cols = [scale_fn(p['mix_w'][0])[:, None] * p['decay_w1'][0],
                scale_fn(p['mix_w'][1])[:, None] * p['decay_w1'][1],
                scale_fn(p['mix_a'][0])[:, None] * p['aaa_a1'][0],
                scale_fn(p['mix_a'][1])[:, None] * p['aaa_a1'][1],
                scale_fn(p['mix_g'])[:, None] * p['gate_g1'],
                jnp.zeros((D_MODEL, GATE_LORA_PAD - GATE_LORA), F32)]
        return jnp.concatenate(cols, axis=1)

    wcat = jnp.concatenate([p['w_in'], lora_cols(lambda m: 1.0 - m), lora_cols(lambda m: 0.5 * m)], axis=1)

    def blockdiag(w):
        z = jnp.zeros_like(w[0])
        return jnp.concatenate([jnp.concatenate([w[0], z], axis=1), jnp.concatenate([z, w[1]], axis=1)], axis=0)

    g2 = jnp.concatenate([p['gate_g2'], jnp.zeros((GATE_LORA_PAD - GATE_LORA, RWKV_WIDTH), F32)], axis=0)
    seg = (jnp.arange(256)[:, None] // HEAD_DIM == jnp.arange(256)[None, :] // HEAD_DIM).astype(BF16)
    pre_w = (
        p['ln1_g'][None, :], wcat.astype(BF16), p['conv_rkv'],
        p['decay_w0'].reshape(1, 2 * RWKV_WIDTH), blockdiag(p['decay_w2']).astype(BF16),
        p['aaa_a0'].reshape(1, 2 * RWKV_WIDTH), blockdiag(p['aaa_a2']).astype(BF16),
        g2.astype(BF16), p['k_k'][None, :], p['k_a'][None, :], p['r_k'].reshape(1, RWKV_WIDTH),
        jnp.tile(p['q_norm_g'], DIFF_WIDTH // DIFF_QK_DIM)[None, :],
        jnp.tile(p['k_norm_g'], DIFF_WIDTH // DIFF_QK_DIM)[None, :], seg,
    )
    post_w = (
        p['lnx_g'][None, :], p['lnx_b'][None, :], seg, p['w_out'].astype(BF16), p['ln2_g'][None, :],
        p['w_gate'].astype(BF16), p['w_up'].astype(BF16), p['w_down'].astype(BF16),
    )
    return pre_w, post_w


def _tile_rows(seq, cap):
    tt = min(seq, cap)
    assert seq % tt == 0 and tt % CHUNK == 0
    return tt


def _trunk(x, meta, pre_w, post_w, lam, subln, tt_pre, tt_post, tq, tk, scan_bb):
    batch, seq, _ = x.shape
    rows = batch * seq
    x2 = x.reshape(rows, D_MODEL)
    tabs = _rope_tables(N_META + seq)
    tabs_meta = tuple(t[:N_META] for t in tabs)
    tabs_real = tuple(t[N_META:] for t in tabs)

    real = _pre_real(x2, meta, tabs_real, pre_w, seq, _tile_rows(seq, tt_pre))
    metao = _pre_meta(x, meta, tabs_meta, pre_w)
    (r, v, kk, lw, kd, b, g, bonus, q, k, vd) = real
    (mr, mv, mkk, mlw, mkd, mb, _, _, _, mk, mvd) = metao

    def r3(a, n):
        return a.reshape(a.shape[:-2] + (batch, n, a.shape[-1]))

    y_f, y_b = _scan((r3(r, seq), r3(v, seq), r3(kk, seq), r3(lw, seq), r3(kd, seq), r3(b, seq)),
                     (r3(mr, N_META), r3(mv, N_META), r3(mkk, N_META), r3(mlw, N_META), r3(mkd, N_META),
                      r3(mb, N_META)), batch, seq, scan_bb)
    yb = _attention(r3(q, seq), r3(k, seq), vd, r3(mk, N_META), mvd, lam, subln,
                    _tile_rows(seq, tq), _tile_rows(seq, tk))
    out = _post(x2, y_f.reshape(rows, RWKV_WIDTH), y_b.reshape(rows, RWKV_WIDTH), bonus, g,
                yb.reshape(rows, DIFF_WIDTH), post_w,
                _tile_rows(seq, tt_post))
    return out.reshape(batch, seq, D_MODEL)


def kernel(x_prompt, x_sample, meta_tokens, ln1_g, w_in, conv_rkv, mix_w, mix_a, mix_g, decay_w0, decay_w1,
           decay_w2, aaa_a0, aaa_a1, aaa_a2, gate_g1, gate_g2, k_k, k_a, r_k, lnx_g, lnx_b, q_norm_g, k_norm_g,
           diff_lambdas, subln_g, w_out, ln2_g, w_gate, w_up, w_down):
    params = dict(ln1_g=ln1_g, w_in=w_in, conv_rkv=conv_rkv, mix_w=mix_w, mix_a=mix_a, mix_g=mix_g,
                  decay_w0=decay_w0, decay_w1=decay_w1, decay_w2=decay_w2, aaa_a0=aaa_a0, aaa_a1=aaa_a1,
                  aaa_a2=aaa_a2, gate_g1=gate_g1, gate_g2=gate_g2, k_k=k_k, k_a=k_a, r_k=r_k, lnx_g=lnx_g,
                  lnx_b=lnx_b, q_norm_g=q_norm_g, k_norm_g=k_norm_g, w_out=w_out, ln2_g=ln2_g, w_gate=w_gate,
                  w_up=w_up, w_down=w_down)
    p = {name: arr[0] for name, arr in params.items()}
    pre_w, post_w = _prepare_weights(p)
    lam = diff_lambdas[0]
    subln = subln_g[0][:, None]
    outs = []
    for x in (x_prompt, x_sample):
        outs.append(_trunk(x, meta_tokens, pre_w, post_w, lam, subln, 256, 256, 512, 256, 2))
    return tuple(outs)
```

```python
import functools
import math
from typing import NamedTuple

import jax
import jax.numpy as jnp
from jax import lax
from jax.experimental import pallas as pl
from jax.experimental.pallas import tpu as pltpu

F32 = jnp.float32
BF16 = jnp.bfloat16

D_MODEL = 1024
N_META = 16
RWKV_HEADS = 8
HEAD_DIM = 64
RWKV_WIDTH = 512
DIFF_HEADS = 4
DIFF_QK_DIM = 64
DIFF_V_DIM = 128
DIFF_WIDTH = 512
IN_WIDTH = 3072
DECAY_LORA = 64
AAA_LORA = 64
GATE_LORA = 160
GATE_LORA_PAD = 256
LORA_BLOCK = 512
CAT_WIDTH = IN_WIDTH + 2 * LORA_BLOCK
ROPE_THETA = 500000.0
ROPE_DIMS = 16
ROPE_HALF = 8
D_FF = 2816
NORM_EPS = 1e-6
GN_EPS = 64e-5
DECAY_SCALE = math.exp(-0.5)
LAMBDA_INIT = 0.8 - 0.6 * math.exp(0.0)

LANES = 128
SUBLANES = 8
HALO = SUBLANES
CHUNK = 64
ROW_TILE = 256
LONG_SEQ = 8192
SCAN_ROWS = 4
VT_ROWS = DIFF_V_DIM + 16
LOG2E = math.log2(math.e)
VMEM_LIMIT = 56 * 1024 * 1024


def _const_spec(shape):
    zeros = (0,) * len(shape)
    return pl.BlockSpec(shape, lambda *_: zeros)


def _seg_sum(x, seg_ref):
    seg = seg_ref[...]
    parts = []
    for c in range(x.shape[1] // 256):
        xs = x[:, 256 * c:256 * (c + 1)].astype(BF16)
        parts.append(jnp.dot(xs, seg, preferred_element_type=F32))
    return jnp.concatenate(parts, axis=1)


def _roll_lanes(x, shift):
    parts = [pltpu.roll(x[:, LANES * c:LANES * (c + 1)], shift, 1) for c in range(x.shape[1] // LANES)]
    return jnp.concatenate(parts, axis=1)


def _pre_compute(xx, tt, tabs, w, outs):
    (cos_ref, sa_ref, sb_ref) = tabs
    (ln1_ref, wcat_ref, conv_ref, w0_ref, w2_ref, a0_ref, a2_ref, g2_ref, kk_ref, ka_ref, rk_ref,
     qg_ref, kg_ref, seg_ref) = w
    (o_r, o_v, o_kk, o_lw, o_kd, o_b, o_g, o_bonus, o_q, o_k, o_vd) = outs
    rows = tt + 2 * HALO

    ms = jnp.mean(xx * xx, axis=-1, keepdims=True)
    n = xx * lax.rsqrt(ms + NORM_EPS) * ln1_ref[...]
    p = jnp.dot(n.astype(BF16), wcat_ref[...], preferred_element_type=F32)

    def prev_rows(a):
        return pltpu.roll(a, 1, 0)[HALO:HALO + tt]

    def next_rows(a):
        return pltpu.roll(a, rows - 1, 0)[HALO:HALO + tt]

    c0 = 3 * RWKV_WIDTH
    rkv_in = p[:, :c0]
    conv = conv_ref[...]
    rkv = conv[0:1] * prev_rows(rkv_in) + conv[1:2] * rkv_in[HALO:HALO + tt] + conv[2:3] * next_rows(rkv_in)
    r = rkv[:, :RWKV_WIDTH]
    k = rkv[:, RWKV_WIDTH:2 * RWKV_WIDTH]
    v = rkv[:, 2 * RWKV_WIDTH:]

    l_self = p[HALO:HALO + tt, IN_WIDTH:IN_WIDTH + LORA_BLOCK]
    l_nbr = p[:, IN_WIDTH + LORA_BLOCK:]
    lora = l_self + prev_rows(l_nbr) + next_rows(l_nbr)

    tw = jnp.tanh(lora[:, 0:128]).astype(BF16)
    zw = jnp.dot(tw, w2_ref[...], preferred_element_type=F32) + w0_ref[...]
    lw = -DECAY_SCALE * jax.nn.sigmoid(zw)
    za = jnp.dot(lora[:, 128:256].astype(BF16), a2_ref[...], preferred_element_type=F32) + a0_ref[...]
    aa = jax.nn.sigmoid(za)
    sg = jax.nn.sigmoid(lora[:, 256:512]).astype(BF16)
    g = jnp.dot(sg, g2_ref[...], preferred_element_type=F32)

    kk = k * kk_ref[...]
    kk = kk * lax.rsqrt(jnp.maximum(_seg_sum(kk * kk, seg_ref), 1e-24))
    k_a = ka_ref[...]
    kd0 = k * (1.0 + (aa[:, :RWKV_WIDTH] - 1.0) * k_a)
    kd1 = k * (1.0 + (aa[:, RWKV_WIDTH:] - 1.0) * k_a)
    bonus = _seg_sum(r * rk_ref[...] * (kd0 + kd1), seg_ref) * v

    o_r[...] = r
    o_v[...] = v
    o_kk[...] = kk
    o_lw[0] = lw[:, :RWKV_WIDTH]
    o_lw[1] = lw[:, RWKV_WIDTH:]
    o_kd[0] = kd0
    o_kd[1] = kd1
    o_b[0] = kk * aa[:, :RWKV_WIDTH]
    o_b[1] = kk * aa[:, RWKV_WIDTH:]
    o_g[...] = g
    o_bonus[...] = bonus

    cos_t = jnp.concatenate([cos_ref[...]] * 4, axis=1)
    sa_t = jnp.concatenate([sa_ref[...]] * 4, axis=1)
    sb_t = jnp.concatenate([sb_ref[...]] * 4, axis=1)

    def qk_norm_rope(xq, g_ref):
        ssq = _seg_sum(xq * xq, seg_ref) * (1.0 / DIFF_QK_DIM)
        xn = xq * lax.rsqrt(ssq + NORM_EPS) * g_ref[...]
        return xn * cos_t + _roll_lanes(xn, LANES - ROPE_HALF) * sa_t + _roll_lanes(xn, ROPE_HALF) * sb_t

    qd = p[HALO:HALO + tt, c0:c0 + DIFF_WIDTH]
    kd = p[HALO:HALO + tt, c0 + DIFF_WIDTH:c0 + 2 * DIFF_WIDTH]
    vd = p[HALO:HALO + tt, c0 + 2 * DIFF_WIDTH:c0 + 3 * DIFF_WIDTH]
    o_q[...] = (qk_norm_rope(qd, qg_ref) * (DIFF_QK_DIM ** -0.5 * LOG2E)).astype(BF16)
    o_k[...] = qk_norm_rope(kd, kg_ref).astype(BF16)
    vd_t = vd.T
    fill = (lax.broadcasted_iota(jnp.int32, (VT_ROWS - DIFF_V_DIM, tt), 0) == 0).astype(F32)
    o_vd[0] = jnp.concatenate(
        [piece for h in range(DIFF_HEADS) for piece in (vd_t[DIFF_V_DIM * h:DIFF_V_DIM * (h + 1)], fill)],
        axis=0).astype(BF16)


def _pre_real_kernel(nt, tt, xm_ref, xp_ref, xn_ref, meta_ref, cos_ref, sa_ref, sb_ref, *rest):
    w, outs = rest[:14], rest[14:]
    j = pl.program_id(0) % nt
    xp = jnp.where(j == 0, meta_ref[N_META - HALO:N_META, :], xp_ref[...])
    xn = jnp.where(j == nt - 1, 0.0, xn_ref[...])
    xx = jnp.concatenate([xp, xm_ref[...], xn], axis=0)
    _pre_compute(xx, tt, (cos_ref, sa_ref, sb_ref), w, outs)


def _pre_meta_kernel(xn_ref, meta_ref, cos_ref, sa_ref, sb_ref, *rest):
    w, outs = rest[:14], rest[14:]
    xx = jnp.concatenate([jnp.zeros((HALO, D_MODEL), F32), meta_ref[...], xn_ref[0]], axis=0)
    _pre_compute(xx, N_META, (cos_ref, sa_ref, sb_ref), w, outs)


def _pre_out_shapes(batch, seq):
    rows = batch * seq
    one = jax.ShapeDtypeStruct((rows, RWKV_WIDTH), F32)
    two = jax.ShapeDtypeStruct((2, rows, RWKV_WIDTH), F32)
    half = jax.ShapeDtypeStruct((rows, DIFF_WIDTH), BF16)
    v_t = jax.ShapeDtypeStruct((batch, DIFF_HEADS * VT_ROWS, seq), BF16)
    return (one, one, one, two, two, two, one, one, half, half, v_t)


def _pre_out_specs(tt, nt):
    one = pl.BlockSpec((tt, RWKV_WIDTH), lambda i: (i, 0))
    two = pl.BlockSpec((2, tt, RWKV_WIDTH), lambda i: (0, i, 0))
    v_t = pl.BlockSpec((1, DIFF_HEADS * VT_ROWS, tt), lambda i: (i // nt, 0, i % nt))
    return (one, one, one, two, two, two, one, one, one, one, v_t)


def _pre_weight_specs():
    shapes = [(1, D_MODEL), (D_MODEL, CAT_WIDTH), (3, 3 * RWKV_WIDTH), (1, 2 * RWKV_WIDTH),
              (2 * DECAY_LORA, 2 * RWKV_WIDTH), (1, 2 * RWKV_WIDTH), (2 * AAA_LORA, 2 * RWKV_WIDTH),
              (GATE_LORA_PAD, RWKV_WIDTH), (1, RWKV_WIDTH), (1, RWKV_WIDTH), (1, RWKV_WIDTH),
              (1, DIFF_WIDTH), (1, DIFF_WIDTH), (256, 256)]
    return [_const_spec(s) for s in shapes]


def _pre_real(x2, meta, tabs_real, weights, seq, tt):
    rows = x2.shape[0]
    nt = seq // tt
    hb = tt // HALO
    last_halo = rows // HALO - 1
    in_specs = [
        pl.BlockSpec((tt, D_MODEL), lambda i: (i, 0)),
        pl.BlockSpec((HALO, D_MODEL), lambda i: (jnp.maximum(i * hb - 1, 0), 0)),
        pl.BlockSpec((HALO, D_MODEL), lambda i: (jnp.minimum((i + 1) * hb, last_halo), 0)),
        _const_spec((N_META, D_MODEL)),
    ] + [pl.BlockSpec((tt, LANES), lambda i: (i % nt, 0))] * 3 + _pre_weight_specs()
    return pl.pallas_call(
        functools.partial(_pre_real_kernel, nt, tt),
        grid=(rows // tt,),
        in_specs=in_specs,
        out_specs=_pre_out_specs(tt, nt),
        out_shape=_pre_out_shapes(rows // seq, seq),
        compiler_params=pltpu.CompilerParams(dimension_semantics=("parallel",), vmem_limit_bytes=VMEM_LIMIT),
        name="pre_real",
    )(x2, x2, x2, meta, *tabs_real, *weights)


def _pre_meta(x3, meta, tabs_meta, weights):
    batch = x3.shape[0]
    in_specs = [
        pl.BlockSpec((1, HALO, D_MODEL), lambda b: (b, 0, 0)),
        _const_spec((N_META, D_MODEL)),
    ] + [_const_spec((N_META, LANES))] * 3 + _pre_weight_specs()
    return pl.pallas_call(
        _pre_meta_kernel,
        grid=(batch,),
        in_specs=in_specs,
        out_specs=_pre_out_specs(N_META, 1),
        out_shape=_pre_out_shapes(batch, N_META),
        compiler_params=pltpu.CompilerParams(dimension_semantics=("parallel",), vmem_limit_bytes=VMEM_LIMIT),
        name="pre_meta",
    )(x3, meta, *tabs_meta, *weights)


def _scan_kernel(ncr, bb, *refs):
    fwd_refs, bwd_refs, meta_refs = refs[0:6], refs[6:12], refs[12:18]
    yf_ref, yb_ref, s_ref = refs[18:21]
    c = pl.program_id(1)
    is_meta = c == 0

    @pl.when(c == 0)
    def _():
        s_ref[...] = jnp.zeros_like(s_ref)

    pad = jnp.zeros((CHUNK - N_META, RWKV_WIDTH), F32)
    row = lax.broadcasted_iota(jnp.int32, (CHUNK, CHUNK), 0)
    colc = lax.broadcasted_iota(jnp.int32, (CHUNK, CHUNK), 1)
    m_incl = {True: (colc <= row).astype(BF16), False: (colc >= row).astype(BF16)}

    def load(refs6, j, fwd):
        vals = []
        for idx, ref in enumerate(refs6):
            x = ref[j] if idx < 3 else ref[0, j]
            if fwd:
                mref = meta_refs[idx]
                meta = mref[j] if idx < 3 else mref[0, j]
                x = jnp.where(is_meta, jnp.concatenate([meta, pad], axis=0), x)
            vals.append(x)
        return vals

    streams = [(load(fwd_refs, j, True), True) for j in range(bb)]
    streams += [(load(bwd_refs, j, False), False) for j in range(bb)]

    prep = []
    for (r, v, kk, lw, kd, b), fwd in streams:
        lw_hi = lw.astype(BF16)
        lw_lo = (lw - lw_hi.astype(F32)).astype(BF16)
        ci = (jnp.dot(m_incl[fwd], lw_hi, preferred_element_type=F32)
              + jnp.dot(m_incl[fwd], lw_lo, preferred_element_type=F32))
        tot = jnp.sum(lw, axis=0, keepdims=True)
        e_inv = jnp.exp(-ci)
        e_rest = jnp.exp(tot - ci)
        prep.append(dict(fwd=fwd, gam=jnp.exp(tot), rt=r * jnp.exp(ci), at=-kk * jnp.exp(ci - lw), kt=kd * e_inv,
                         bt=b * e_inv, kh=kd * e_rest, bh=b * e_rest, v=v))

    rowp = lax.broadcasted_iota(jnp.int32, (CHUNK, LANES), 0)
    lanep = lax.broadcasted_iota(jnp.int32, (CHUNK, LANES), 1)
    left = lanep < HEAD_DIM
    eye = (lanep % CHUNK == rowp).astype(F32)
    row2 = lax.broadcasted_iota(jnp.int32, (CHUNK, 2 * LANES), 0)
    src2 = lax.broadcasted_iota(jnp.int32, (CHUNK, 2 * LANES), 1) % CHUNK
    strict = {True: src2 < row2, False: src2 > row2}
    incl = {True: src2 <= row2, False: src2 >= row2}

    def bd(x):
        return jnp.concatenate([jnp.where(left, x, 0.0), jnp.where(left, 0.0, x)], axis=0).astype(BF16)

    chains = [(si, pr) for si in range(len(prep)) for pr in range(RWKV_HEADS // 2)]

    def pslice(name, si, pr):
        return prep[si][name][:, LANES * pr:LANES * (pr + 1)]

    def stage(fn):
        return {ch: fn(*ch) for ch in chains}

    def mm(a, b):
        return jnp.dot(a.astype(BF16), b, preferred_element_type=F32)

    fwd_of = {ch: prep[ch[0]]['fwd'] for ch in chains}
    contract = (((1,), (1,)), ((), ()))
    gmat = stage(lambda si, pr: lax.dot_general(
        jnp.concatenate([pslice('at', si, pr), pslice('rt', si, pr)], axis=0).astype(BF16),
        jnp.concatenate([bd(pslice('bt', si, pr)), bd(pslice('kt', si, pr))], axis=0), contract,
        preferred_element_type=F32))
    g_strict = stage(lambda si, pr: jnp.where(strict[fwd_of[si, pr]], gmat[si, pr][:CHUNK], 0.0))
    bdv = stage(lambda si, pr: bd(pslice('v', si, pr)))
    akv = stage(lambda si, pr: mm(g_strict[si, pr][:, LANES:], bdv[si, pr]))
    tmat = stage(lambda si, pr: eye + g_strict[si, pr][:, :LANES])
    pmat = stage(lambda si, pr: mm(g_strict[si, pr][:, :LANES], bd(g_strict[si, pr][:, :LANES])))
    for _ in range(4):
        res = stage(lambda si, pr: mm(pmat[si, pr], jnp.concatenate([bd(tmat[si, pr]), bd(pmat[si, pr])], axis=1)))
        tmat = stage(lambda si, pr: tmat[si, pr] + res[si, pr][:, :LANES])
        pmat = stage(lambda si, pr: res[si, pr][:, LANES:])
    tmat = stage(lambda si, pr: tmat[si, pr] + mm(pmat[si, pr], bd(tmat[si, pr])))
    wu = stage(lambda si, pr: mm(tmat[si, pr], jnp.concatenate([bd(pslice('at', si, pr)), bd(akv[si, pr])], axis=1)))
    rows_p = lax.broadcasted_iota(jnp.int32, (LANES, LANES), 0)
    lanes_p = lax.broadcasted_iota(jnp.int32, (LANES, LANES), 1)
    same_head = (rows_p // HEAD_DIM) == (lanes_p // HEAD_DIM)
    diag_p = rows_p == lanes_p
    s_old = stage(lambda si, pr: s_ref[si, pr].astype(BF16))
    su = stage(lambda si, pr: mm(jnp.concatenate([wu[si, pr][:, :LANES], pslice('rt', si, pr)], axis=0),
                                 s_old[si, pr]))
    umat = stage(lambda si, pr: su[si, pr][:CHUNK] + wu[si, pr][:, LANES:])
    y_out = stage(lambda si, pr: su[si, pr][CHUNK:] + mm(
        jnp.where(incl[fwd_of[si, pr]], gmat[si, pr][CHUNK:], 0.0),
        jnp.concatenate([bd(umat[si, pr]), bdv[si, pr]], axis=0)))

    def next_state(si, pr):
        bk_t = jnp.concatenate([pslice('bh', si, pr), pslice('kh', si, pr)], axis=0).T
        lhs = jnp.concatenate([bk_t, jnp.where(diag_p, pslice('gam', si, pr), 0.0)], axis=1)
        rhs = jnp.concatenate([umat[si, pr].astype(BF16), pslice('v', si, pr).astype(BF16), s_old[si, pr]], axis=0)
        return jnp.where(same_head, mm(lhs, rhs), 0.0)

    new_state = stage(next_state)

    for j in range(bb):
        for pr in range(RWKV_HEADS // 2):
            s_ref[j, pr] = new_state[j, pr]
            yf_ref[j, :, LANES * pr:LANES * (pr + 1)] = y_out[j, pr]

    @pl.when(c < ncr)
    def _():
        for j in range(bb):
            for pr in range(RWKV_HEADS // 2):
                s_ref[bb + j, pr] = new_state[bb + j, pr]
                yb_ref[j, :, LANES * pr:LANES * (pr + 1)] = y_out[bb + j, pr]


def _scan(real, meta, batch, seq, bb):
    ncr = seq // CHUNK
    assert batch % bb == 0

    def fidx(c):
        return jnp.maximum(c - 1, 0)

    def bidx(c):
        return jnp.maximum(ncr - 1 - c, 0)

    def specs(idx, d):
        one = pl.BlockSpec((bb, CHUNK, RWKV_WIDTH), lambda g, c: (g, idx(c), 0))
        two = pl.BlockSpec((1, bb, CHUNK, RWKV_WIDTH), lambda g, c: (d, g, idx(c), 0))
        return [one, one, one, two, two, two]

    mone = pl.BlockSpec((bb, N_META, RWKV_WIDTH), lambda g, c: (g, 0, 0))
    mtwo = pl.BlockSpec((1, bb, N_META, RWKV_WIDTH), lambda g, c: (0, g, 0, 0))
    y_shape = jax.ShapeDtypeStruct((batch, seq, RWKV_WIDTH), F32)
    return pl.pallas_call(
        functools.partial(_scan_kernel, ncr, bb),
        grid=(batch // bb, ncr + 1),
        in_specs=specs(fidx, 0) + specs(bidx, 1) + [mone, mone, mone, mtwo, mtwo, mtwo],
        out_specs=(pl.BlockSpec((bb, CHUNK, RWKV_WIDTH), lambda g, c: (g, fidx(c), 0)),
                   pl.BlockSpec((bb, CHUNK, RWKV_WIDTH), lambda g, c: (g, bidx(c), 0))),
        out_shape=(y_shape, y_shape),
        scratch_shapes=[pltpu.VMEM((2 * bb, RWKV_HEADS // 2, LANES, LANES), F32)],
        compiler_params=pltpu.CompilerParams(
            dimension_semantics=("parallel", "arbitrary"), vmem_limit_bytes=VMEM_LIMIT),
        name="wkv_scan",
    )(*real, *real, *meta)


def _attn_kernel(nkb, tk, group, q_ref, k_ref, vt_ref, km_ref, vmt_ref, lam_ref, sg_ref, o_ref):
    q = q_ref[0]
    tq = q.shape[0]
    lane = lax.broadcasted_iota(jnp.int32, (tq, LANES), 1)
    zero = jnp.zeros_like(q)
    qc = (jnp.where(lane < DIFF_QK_DIM, q, zero), jnp.where(lane >= DIFF_QK_DIM, q, zero))
    contract = (((1,), (1,)), ((), ()))

    def blocks(kbs, vtbs, state):
        scores = [[lax.dot_general(kb, qc[comp], contract, preferred_element_type=F32) for comp in range(2)]
                  for kb in kbs]
        state = list(state)
        for s_pair, vtb in zip(scores, vtbs):
            for comp in range(2):
                m, acc = state[comp]
                s = s_pair[comp]
                m_new = jnp.maximum(m, jnp.max(s, axis=0, keepdims=True))
                pexp = jnp.exp2(s - m_new).astype(BF16)
                acc_new = jnp.exp2(m - m_new) * acc + jnp.dot(vtb, pexp, preferred_element_type=F32)
                state[comp] = (m_new, acc_new)
        return tuple(state)

    def group_at(i):
        starts = [pl.multiple_of((i * group + j) * tk, tk) for j in range(group)]
        return ([k_ref[0, pl.ds(st_j, tk), :] for st_j in starts],
                [vt_ref[0, :, pl.ds(st_j, tk)] for st_j in starts])

    init = tuple((jnp.full((1, tq), -jnp.inf, F32), jnp.zeros((VT_ROWS, tq), F32)) for _ in range(2))
    kbs0, vtbs0 = group_at(0)
    state = blocks([km_ref[0]] + kbs0, [vmt_ref[0]] + vtbs0, init)
    state = lax.fori_loop(1, nkb // group, lambda i, st: blocks(*group_at(i), st), state)
    lam_v = lam_ref[...]
    lam = (jnp.exp(jnp.sum(lam_v[0:1] * lam_v[1:2], axis=-1, keepdims=True))
           - jnp.exp(jnp.sum(lam_v[2:3] * lam_v[3:4], axis=-1, keepdims=True)) + LAMBDA_INIT)
    (_, acc0), (_, acc1) = state
    a0, l0 = acc0[:DIFF_V_DIM], acc0[DIFF_V_DIM:DIFF_V_DIM + 1]
    a1, l1 = acc1[:DIFF_V_DIM], acc1[DIFF_V_DIM:DIFF_V_DIM + 1]
    o = a0 / l0 - lam * (a1 / l1)
    o = o * lax.rsqrt(jnp.mean(o * o, axis=0, keepdims=True) + NORM_EPS) * sg_ref[...]
    o_ref[0] = (o * (1.0 - LAMBDA_INIT)).T


def _attention(q, k, vt, km, vmt, lam, subln, tiles):
    batch, seq, _ = q.shape
    tq, tk = tiles.tq, tiles.tk
    nkb = seq // tk
    return pl.pallas_call(
        functools.partial(_attn_kernel, nkb, tk, math.gcd(nkb, tiles.attn_group)),
        grid=(batch, DIFF_HEADS, seq // tq),
        in_specs=[
            pl.BlockSpec((1, tq, LANES), lambda b, h, i: (b, i, h)),
            pl.BlockSpec((1, seq, LANES), lambda b, h, i: (b, 0, h)),
            pl.BlockSpec((1, VT_ROWS, seq), lambda b, h, i: (b, h, 0)),
            pl.BlockSpec((1, N_META, LANES), lambda b, h, i: (b, 0, h)),
            pl.BlockSpec((1, VT_ROWS, N_META), lambda b, h, i: (b, h, 0)),
            _const_spec((4, DIFF_QK_DIM)),
            _const_spec((DIFF_V_DIM, 1)),
        ],
        out_specs=pl.BlockSpec((1, tq, LANES), lambda b, h, i: (b, i, h)),
        out_shape=jax.ShapeDtypeStruct((batch, seq, DIFF_WIDTH), F32),
        compiler_params=pltpu.CompilerParams(
            dimension_semantics=("parallel", "parallel", "parallel"), vmem_limit_bytes=VMEM_LIMIT),
        name="diff_attn",
    )(q, k, vt, km, vmt, lam, subln)


def _post_kernel(x_ref, yf_ref, ybw_ref, bonus_ref, g_ref, yb_ref, lng_ref, lnb_ref, seg_ref, wout_ref, ln2_ref,
                 wg_ref, wu_ref, wd_ref, o_ref):
    ys = yf_ref[...] + ybw_ref[...]
    inv_n = 1.0 / HEAD_DIM
    mu = _seg_sum(ys, seg_ref) * inv_n
    dev = ys - mu
    var = _seg_sum(dev * dev, seg_ref) * inv_n
    yn = dev * lax.rsqrt(var + GN_EPS) * lng_ref[...] + lnb_ref[...]
    ya = (yn + bonus_ref[...]) * g_ref[...]
    mix = jnp.concatenate([ya, yb_ref[...]], axis=1).astype(BF16)
    x1 = x_ref[...] + jnp.dot(mix, wout_ref[...], preferred_element_type=F32)
    n2 = x1 * lax.rsqrt(jnp.mean(x1 * x1, axis=-1, keepdims=True) + NORM_EPS) * ln2_ref[...]
    n2 = n2.astype(BF16)
    gate = jnp.dot(n2, wg_ref[...], preferred_element_type=F32)
    up = jnp.dot(n2, wu_ref[...], preferred_element_type=F32)
    hid = (gate * jax.nn.sigmoid(gate) * up).astype(BF16)
    o_ref[...] = x1 + jnp.dot(hid, wd_ref[...], preferred_element_type=F32)


def _post(x2, y_f, y_b, bonus, g, yb, weights, tt):
    rows = x2.shape[0]
    row512 = pl.BlockSpec((tt, RWKV_WIDTH), lambda i: (i, 0))
    in_specs = [
        pl.BlockSpec((tt, D_MODEL), lambda i: (i, 0)),
        row512, row512, row512, row512, row512,
        _const_spec((1, RWKV_WIDTH)), _const_spec((1, RWKV_WIDTH)), _const_spec((256, 256)),
        _const_spec((D_MODEL, D_MODEL)), _const_spec((1, D_MODEL)),
        _const_spec((D_MODEL, D_FF)), _const_spec((D_MODEL, D_FF)), _const_spec((D_FF, D_MODEL)),
    ]
    return pl.pallas_call(
        _post_kernel,
        grid=(rows // tt,),
        in_specs=in_specs,
        out_specs=pl.BlockSpec((tt, D_MODEL), lambda i: (i, 0)),
        out_shape=jax.ShapeDtypeStruct((rows, D_MODEL), F32),
        compiler_params=pltpu.CompilerParams(dimension_semantics=("parallel",), vmem_limit_bytes=VMEM_LIMIT),
        name="post",
    )(x2, y_f, y_b, bonus, g, yb, *weights)


def _rope_tables(total):
    freqs = ROPE_THETA ** (-jnp.arange(0, ROPE_DIMS, 2, dtype=F32) / ROPE_DIMS)
    ang = jnp.arange(total, dtype=F32)[:, None] * freqs[None, :]
    cos, sin = jnp.cos(ang), jnp.sin(ang)
    ones = jnp.ones((total, DIFF_QK_DIM - ROPE_DIMS), F32)
    zeros = jnp.zeros((total, ROPE_HALF), F32)
    zrest = jnp.zeros((total, DIFF_QK_DIM - ROPE_DIMS), F32)
    cos64 = jnp.concatenate([cos, cos, ones], axis=1)
    sa64 = jnp.concatenate([-sin, zeros, zrest], axis=1)
    sb64 = jnp.concatenate([zeros, sin, zrest], axis=1)
    return tuple(jnp.concatenate([t, t], axis=1) for t in (cos64, sa64, sb64))


def _prepare_weights(p):
    def lora_cols(scale_fn):
        cols = [scale_fn(p['mix_w'][0])[:, None] * p['decay_w1'][0],
                scale_fn(p['mix_w'][1])[:, None] * p['decay_w1'][1],
                scale_fn(p['mix_a'][0])[:, None] * p['aaa_a1'][0],
                scale_fn(p['mix_a'][1])[:, None] * p['aaa_a1'][1],
                scale_fn(p['mix_g'])[:, None] * p['gate_g1'],
                jnp.zeros((D_MODEL, GATE_LORA_PAD - GATE_LORA), F32)]
        return jnp.concatenate(cols, axis=1)

    wcat = jnp.concatenate([p['w_in'], lora_cols(lambda m: 1.0 - m), lora_cols(lambda m: 0.5 * m)], axis=1)

    def blockdiag(w):
        z = jnp.zeros_like(w[0])
        return jnp.concatenate([jnp.concatenate([w[0], z], axis=1), jnp.concatenate([z, w[1]], axis=1)], axis=0)

    g2 = jnp.concatenate([p['gate_g2'], jnp.zeros((GATE_LORA_PAD - GATE_LORA, RWKV_WIDTH), F32)], axis=0)
    seg = (jnp.arange(256)[:, None] // HEAD_DIM == jnp.arange(256)[None, :] // HEAD_DIM).astype(BF16)
    pre_w = (
        p['ln1_g'][None, :], wcat.astype(BF16), p['conv_rkv'],
        p['decay_w0'].reshape(1, 2 * RWKV_WIDTH), blockdiag(p['decay_w2']).astype(BF16),
        p['aaa_a0'].reshape(1, 2 * RWKV_WIDTH), blockdiag(p['aaa_a2']).astype(BF16),
        g2.astype(BF16), p['k_k'][None, :], p['k_a'][None, :], p['r_k'].reshape(1, RWKV_WIDTH),
        jnp.tile(p['q_norm_g'], DIFF_WIDTH // DIFF_QK_DIM)[None, :],
        jnp.tile(p['k_norm_g'], DIFF_WIDTH // DIFF_QK_DIM)[None, :], seg,
    )
    post_w = (
        p['lnx_g'][None, :], p['lnx_b'][None, :], seg, p['w_out'].astype(BF16), p['ln2_g'][None, :],
        p['w_gate'].astype(BF16), p['w_up'].astype(BF16), p['w_down'].astype(BF16),
    )
    return pre_w, post_w


def _tile_rows(seq, cap):
    tt = min(seq, cap)
    assert seq % tt == 0 and tt % CHUNK == 0
    return tt


class Tiles(NamedTuple):
    pre: int
    post: int
    tq: int
    tk: int
    attn_group: int
    scan_bb: int


def _tiles(batch, seq):
    row_tile = _tile_rows(seq, ROW_TILE)
    long_seq = seq >= LONG_SEQ
    return Tiles(pre=row_tile, post=row_tile,
                 tq=_tile_rows(seq, 256 if long_seq else 512), tk=_tile_rows(seq, 256),
                 attn_group=16 if long_seq else 8, scan_bb=math.gcd(batch, SCAN_ROWS))


def _trunk(x, meta, pre_w, post_w, lam, subln, tiles):
    batch, seq, _ = x.shape
    rows = batch * seq
    x2 = x.reshape(rows, D_MODEL)
    tabs = _rope_tables(N_META + seq)
    tabs_meta = tuple(t[:N_META] for t in tabs)
    tabs_real = tuple(t[N_META:] for t in tabs)

    real = _pre_real(x2, meta, tabs_real, pre_w, seq, tiles.pre)
    metao = _pre_meta(x, meta, tabs_meta, pre_w)
    (r, v, kk, lw, kd, b, g, bonus, q, k, vd) = real
    (mr, mv, mkk, mlw, mkd, mb, _, _, _, mk, mvd) = metao

    def r3(a, n):
        return a.reshape(a.shape[:-2] + (batch, n, a.shape[-1]))

    y_f, y_b = _scan((r3(r, seq), r3(v, seq), r3(kk, seq), r3(lw, seq), r3(kd, seq), r3(b, seq)),
                     (r3(mr, N_META), r3(mv, N_META), r3(mkk, N_META), r3(mlw, N_META), r3(mkd, N_META),
                      r3(mb, N_META)), batch, seq, tiles.scan_bb)
    yb = _attention(r3(q, seq), r3(k, seq), vd, r3(mk, N_META), mvd, lam, subln, tiles)
    out = _post(x2, y_f.reshape(rows, RWKV_WIDTH), y_b.reshape(rows, RWKV_WIDTH), bonus, g,
                yb.reshape(rows, DIFF_WIDTH), post_w, tiles.post)
    return out.reshape(batch, seq, D_MODEL)


def kernel(x_prompt, x_sample, meta_tokens, ln1_g, w_in, conv_rkv, mix_w, mix_a, mix_g, decay_w0, decay_w1,
           decay_w2, aaa_a0, aaa_a1, aaa_a2, gate_g1, gate_g2, k_k, k_a, r_k, lnx_g, lnx_b, q_norm_g, k_norm_g,
           diff_lambdas, subln_g, w_out, ln2_g, w_gate, w_up, w_down):
    params = dict(ln1_g=ln1_g, w_in=w_in, conv_rkv=conv_rkv, mix_w=mix_w, mix_a=mix_a, mix_g=mix_g,
                  decay_w0=decay_w0, decay_w1=decay_w1, decay_w2=decay_w2, aaa_a0=aaa_a0, aaa_a1=aaa_a1,
                  aaa_a2=aaa_a2, gate_g1=gate_g1, gate_g2=gate_g2, k_k=k_k, k_a=k_a, r_k=r_k, lnx_g=lnx_g,
                  lnx_b=lnx_b, q_norm_g=q_norm_g, k_norm_g=k_norm_g, w_out=w_out, ln2_g=ln2_g, w_gate=w_gate,
                  w_up=w_up, w_down=w_down)
    p = {name: arr[0] for name, arr in params.items()}
    pre_w, post_w = _prepare_weights(p)
    lam = diff_lambdas[0]
    subln = subln_g[0][:, None]
    outs = []
    for x in (x_prompt, x_sample):
        outs.append(_trunk(x, meta_tokens, pre_w, post_w, lam, subln, _tiles(x.shape[0], x.shape[1])))
    return tuple(outs)
```

```python
import functools
import math
from typing import NamedTuple

import jax
import jax.numpy as jnp
from jax import lax
from jax.experimental import pallas as pl
from jax.experimental.pallas import tpu as pltpu

F32 = jnp.float32
BF16 = jnp.bfloat16

D_MODEL = 1024
N_META = 16
RWKV_HEADS = 8
HEAD_DIM = 64
RWKV_WIDTH = 512
DIFF_HEADS = 4
DIFF_QK_DIM = 64
DIFF_V_DIM = 128
DIFF_WIDTH = 512
IN_WIDTH = 3072
DECAY_LORA = 64
AAA_LORA = 64
GATE_LORA = 160
GATE_LORA_PAD = 256
LORA_BLOCK = 512
CAT_WIDTH = IN_WIDTH + 2 * LORA_BLOCK
ROPE_THETA = 500000.0
ROPE_DIMS = 16
ROPE_HALF = 8
D_FF = 2816
NORM_EPS = 1e-6
GN_EPS = 64e-5
KK_NORM_FLOOR = 1e-12
DECAY_SCALE = math.exp(-0.5)
LAMBDA_INIT = 0.8 - 0.6 * math.exp(0.0)

LANES = 128
SUBLANES = 8
HALO = SUBLANES
CHUNK = 64
MAX_CONST_SHIFT = 60.0
ROW_TILE = 256
LONG_SEQ = 8192
SCAN_ROWS = 4
VT_ROWS = DIFF_V_DIM + 16
LOG2E = math.log2(math.e)
VMEM_LIMIT = 56 * 1024 * 1024


def _const_spec(shape):
    zeros = (0,) * len(shape)
    return pl.BlockSpec(shape, lambda *_: zeros)


def _seg_sum(x, seg_ref):
    seg = seg_ref[...]
    parts = []
    for c in range(x.shape[1] // 256):
        xs = x[:, 256 * c:256 * (c + 1)].astype(BF16)
        parts.append(jnp.dot(xs, seg, preferred_element_type=F32))
    return jnp.concatenate(parts, axis=1)


def _roll_lanes(x, shift):
    parts = [pltpu.roll(x[:, LANES * c:LANES * (c + 1)], shift, 1) for c in range(x.shape[1] // LANES)]
    return jnp.concatenate(parts, axis=1)


def _pre_compute(xx, tt, tabs, w, outs):
    (cos_ref, sa_ref, sb_ref) = tabs
    (ln1_ref, wcat_ref, conv_ref, w0_ref, w2_ref, a0_ref, a2_ref, g2_ref, kk_ref, ka_ref, rk_ref,
     qg_ref, kg_ref, seg_ref) = w
    (o_r, o_v, o_kk, o_lw, o_kd, o_b, o_g, o_bonus, o_q, o_k, o_vd) = outs
    rows = tt + 2 * HALO

    ms = jnp.mean(xx * xx, axis=-1, keepdims=True)
    n = xx * lax.rsqrt(ms + NORM_EPS) * ln1_ref[...]
    p = jnp.dot(n.astype(BF16), wcat_ref[...], preferred_element_type=F32)

    def prev_rows(a):
        return pltpu.roll(a, 1, 0)[HALO:HALO + tt]

    def next_rows(a):
        return pltpu.roll(a, rows - 1, 0)[HALO:HALO + tt]

    c0 = 3 * RWKV_WIDTH
    rkv_in = p[:, :c0]
    conv = conv_ref[...]
    rkv = conv[0:1] * prev_rows(rkv_in) + conv[1:2] * rkv_in[HALO:HALO + tt] + conv[2:3] * next_rows(rkv_in)
    r = rkv[:, :RWKV_WIDTH]
    k = rkv[:, RWKV_WIDTH:2 * RWKV_WIDTH]
    v = rkv[:, 2 * RWKV_WIDTH:]

    l_self = p[HALO:HALO + tt, IN_WIDTH:IN_WIDTH + LORA_BLOCK]
    l_nbr = p[:, IN_WIDTH + LORA_BLOCK:]
    lora = l_self + prev_rows(l_nbr) + next_rows(l_nbr)

    tw = jnp.tanh(lora[:, 0:128]).astype(BF16)
    zw = jnp.dot(tw, w2_ref[...], preferred_element_type=F32) + w0_ref[...]
    lw = -DECAY_SCALE * jax.nn.sigmoid(zw)
    za = jnp.dot(lora[:, 128:256].astype(BF16), a2_ref[...], preferred_element_type=F32) + a0_ref[...]
    aa = jax.nn.sigmoid(za)
    sg = jax.nn.sigmoid(lora[:, 256:512]).astype(BF16)
    g = jnp.dot(sg, g2_ref[...], preferred_element_type=F32)

    kk = k * kk_ref[...]
    kk = kk * lax.rsqrt(jnp.maximum(_seg_sum(kk * kk, seg_ref), KK_NORM_FLOOR ** 2))
    k_a = ka_ref[...]
    kd0 = k * (1.0 + (aa[:, :RWKV_WIDTH] - 1.0) * k_a)
    kd1 = k * (1.0 + (aa[:, RWKV_WIDTH:] - 1.0) * k_a)
    bonus = _seg_sum(r * rk_ref[...] * (kd0 + kd1), seg_ref) * v

    o_r[...] = r
    o_v[...] = v
    o_kk[...] = kk
    o_lw[0] = lw[:, :RWKV_WIDTH]
    o_lw[1] = lw[:, RWKV_WIDTH:]
    o_kd[0] = kd0
    o_kd[1] = kd1
    o_b[0] = kk * aa[:, :RWKV_WIDTH]
    o_b[1] = kk * aa[:, RWKV_WIDTH:]
    o_g[...] = g
    o_bonus[...] = bonus

    cos_t = jnp.concatenate([cos_ref[...]] * 4, axis=1)
    sa_t = jnp.concatenate([sa_ref[...]] * 4, axis=1)
    sb_t = jnp.concatenate([sb_ref[...]] * 4, axis=1)

    def qk_norm_rope(xq, g_ref):
        ssq = _seg_sum(xq * xq, seg_ref) * (1.0 / DIFF_QK_DIM)
        xn = xq * lax.rsqrt(ssq + NORM_EPS) * g_ref[...]
        return xn * cos_t + _roll_lanes(xn, LANES - ROPE_HALF) * sa_t + _roll_lanes(xn, ROPE_HALF) * sb_t

    qd = p[HALO:HALO + tt, c0:c0 + DIFF_WIDTH]
    kd = p[HALO:HALO + tt, c0 + DIFF_WIDTH:c0 + 2 * DIFF_WIDTH]
    vd = p[HALO:HALO + tt, c0 + 2 * DIFF_WIDTH:c0 + 3 * DIFF_WIDTH]
    o_q[...] = (qk_norm_rope(qd, qg_ref) * (DIFF_QK_DIM ** -0.5 * LOG2E)).astype(BF16)
    o_k[...] = qk_norm_rope(kd, kg_ref).astype(BF16)
    vd_t = vd.T
    fill = (lax.broadcasted_iota(jnp.int32, (VT_ROWS - DIFF_V_DIM, tt), 0) == 0).astype(F32)
    o_vd[0] = jnp.concatenate(
        [piece for h in range(DIFF_HEADS) for piece in (vd_t[DIFF_V_DIM * h:DIFF_V_DIM * (h + 1)], fill)],
        axis=0).astype(BF16)


def _pre_real_kernel(nt, tt, xm_ref, xp_ref, xn_ref, meta_ref, cos_ref, sa_ref, sb_ref, *rest):
    w, outs = rest[:14], rest[14:]
    j = pl.program_id(0) % nt
    xp = jnp.where(j == 0, meta_ref[N_META - HALO:N_META, :], xp_ref[...])
    xn = jnp.where(j == nt - 1, 0.0, xn_ref[...])
    xx = jnp.concatenate([xp, xm_ref[...], xn], axis=0)
    _pre_compute(xx, tt, (cos_ref, sa_ref, sb_ref), w, outs)


def _pre_meta_kernel(xn_ref, meta_ref, cos_ref, sa_ref, sb_ref, *rest):
    w, outs = rest[:14], rest[14:]
    xx = jnp.concatenate([jnp.zeros((HALO, D_MODEL), F32), meta_ref[...], xn_ref[0]], axis=0)
    _pre_compute(xx, N_META, (cos_ref, sa_ref, sb_ref), w, outs)


def _pre_out_shapes(batch, seq):
    rows = batch * seq
    one = jax.ShapeDtypeStruct((rows, RWKV_WIDTH), F32)
    two = jax.ShapeDtypeStruct((2, rows, RWKV_WIDTH), F32)
    half = jax.ShapeDtypeStruct((rows, DIFF_WIDTH), BF16)
    v_t = jax.ShapeDtypeStruct((batch, DIFF_HEADS * VT_ROWS, seq), BF16)
    return (one, one, one, two, two, two, one, one, half, half, v_t)


def _pre_out_specs(tt, nt):
    one = pl.BlockSpec((tt, RWKV_WIDTH), lambda i: (i, 0))
    two = pl.BlockSpec((2, tt, RWKV_WIDTH), lambda i: (0, i, 0))
    v_t = pl.BlockSpec((1, DIFF_HEADS * VT_ROWS, tt), lambda i: (i // nt, 0, i % nt))
    return (one, one, one, two, two, two, one, one, one, one, v_t)


def _pre_weight_specs():
    shapes = [(1, D_MODEL), (D_MODEL, CAT_WIDTH), (3, 3 * RWKV_WIDTH), (1, 2 * RWKV_WIDTH),
              (2 * DECAY_LORA, 2 * RWKV_WIDTH), (1, 2 * RWKV_WIDTH), (2 * AAA_LORA, 2 * RWKV_WIDTH),
              (GATE_LORA_PAD, RWKV_WIDTH), (1, RWKV_WIDTH), (1, RWKV_WIDTH), (1, RWKV_WIDTH),
              (1, DIFF_WIDTH), (1, DIFF_WIDTH), (256, 256)]
    return [_const_spec(s) for s in shapes]


def _pre_real(x2, meta, tabs_real, weights, seq, tt):
    rows = x2.shape[0]
    nt = seq // tt
    hb = tt // HALO
    last_halo = rows // HALO - 1
    in_specs = [
        pl.BlockSpec((tt, D_MODEL), lambda i: (i, 0)),
        pl.BlockSpec((HALO, D_MODEL), lambda i: (jnp.maximum(i * hb - 1, 0), 0)),
        pl.BlockSpec((HALO, D_MODEL), lambda i: (jnp.minimum((i + 1) * hb, last_halo), 0)),
        _const_spec((N_META, D_MODEL)),
    ] + [pl.BlockSpec((tt, LANES), lambda i: (i % nt, 0))] * 3 + _pre_weight_specs()
    return pl.pallas_call(
        functools.partial(_pre_real_kernel, nt, tt),
        grid=(rows // tt,),
        in_specs=in_specs,
        out_specs=_pre_out_specs(tt, nt),
        out_shape=_pre_out_shapes(rows // seq, seq),
        compiler_params=pltpu.CompilerParams(dimension_semantics=("parallel",), vmem_limit_bytes=VMEM_LIMIT),
        name="pre_real",
    )(x2, x2, x2, meta, *tabs_real, *weights)


def _pre_meta(x3, meta, tabs_meta, weights):
    batch = x3.shape[0]
    in_specs = [
        pl.BlockSpec((1, HALO, D_MODEL), lambda b: (b, 0, 0)),
        _const_spec((N_META, D_MODEL)),
    ] + [_const_spec((N_META, LANES))] * 3 + _pre_weight_specs()
    return pl.pallas_call(
        _pre_meta_kernel,
        grid=(batch,),
        in_specs=in_specs,
        out_specs=_pre_out_specs(N_META, 1),
        out_shape=_pre_out_shapes(batch, N_META),
        compiler_params=pltpu.CompilerParams(dimension_semantics=("parallel",), vmem_limit_bytes=VMEM_LIMIT),
        name="pre_meta",
    )(x3, meta, *tabs_meta, *weights)


def _scan_kernel(ncr, bb, *refs):
    fwd_refs, bwd_refs, meta_refs = refs[0:6], refs[6:12], refs[12:18]
    yf_ref, yb_ref, s_ref = refs[18:21]
    c = pl.program_id(1)
    is_meta = c == 0

    @pl.when(c == 0)
    def _():
        s_ref[...] = jnp.zeros_like(s_ref)

    pad = jnp.zeros((CHUNK - N_META, RWKV_WIDTH), F32)
    row = lax.broadcasted_iota(jnp.int32, (CHUNK, CHUNK), 0)
    colc = lax.broadcasted_iota(jnp.int32, (CHUNK, CHUNK), 1)
    m_incl = {True: (colc <= row).astype(BF16), False: (colc >= row).astype(BF16)}

    def load(refs6, j, fwd):
        vals = []
        for idx, ref in enumerate(refs6):
            x = ref[j] if idx < 3 else ref[0, j]
            if fwd:
                mref = meta_refs[idx]
                meta = mref[j] if idx < 3 else mref[0, j]
                x = jnp.where(is_meta, jnp.concatenate([meta, pad], axis=0), x)
            vals.append(x)
        return vals

    streams = [(load(fwd_refs, j, True), True) for j in range(bb)]
    streams += [(load(bwd_refs, j, False), False) for j in range(bb)]

    prep = []
    for (r, v, kk, lw, kd, b), fwd in streams:
        lw_hi = lw.astype(BF16)
        lw_lo = (lw - lw_hi.astype(F32)).astype(BF16)
        ci = (jnp.dot(m_incl[fwd], lw_hi, preferred_element_type=F32)
              + jnp.dot(m_incl[fwd], lw_lo, preferred_element_type=F32))
        tot = jnp.sum(lw, axis=0, keepdims=True)
        e_inv = jnp.exp(-ci)
        e_rest = jnp.exp(tot - ci)
        prep.append(dict(fwd=fwd, gam=jnp.exp(tot), rt=r * jnp.exp(ci), at=-kk * jnp.exp(ci - lw), kt=kd * e_inv,
                         bt=b * e_inv, kh=kd * e_rest, bh=b * e_rest, v=v))

    rowp = lax.broadcasted_iota(jnp.int32, (CHUNK, LANES), 0)
    lanep = lax.broadcasted_iota(jnp.int32, (CHUNK, LANES), 1)
    left = lanep < HEAD_DIM
    eye = (lanep % CHUNK == rowp).astype(F32)
    row2 = lax.broadcasted_iota(jnp.int32, (CHUNK, 2 * LANES), 0)
    src2 = lax.broadcasted_iota(jnp.int32, (CHUNK, 2 * LANES), 1) % CHUNK
    strict = {True: src2 < row2, False: src2 > row2}
    incl = {True: src2 <= row2, False: src2 >= row2}

    def bd(x):
        return jnp.concatenate([jnp.where(left, x, 0.0), jnp.where(left, 0.0, x)], axis=0).astype(BF16)

    chains = [(si, pr) for si in range(len(prep)) for pr in range(RWKV_HEADS // 2)]

    def pslice(name, si, pr):
        return prep[si][name][:, LANES * pr:LANES * (pr + 1)]

    def stage(fn):
        return {ch: fn(*ch) for ch in chains}

    def mm(a, b):
        return jnp.dot(a.astype(BF16), b, preferred_element_type=F32)

    fwd_of = {ch: prep[ch[0]]['fwd'] for ch in chains}
    contract = (((1,), (1,)), ((), ()))
    gmat = stage(lambda si, pr: lax.dot_general(
        jnp.concatenate([pslice('at', si, pr), pslice('rt', si, pr)], axis=0).astype(BF16),
        jnp.concatenate([bd(pslice('bt', si, pr)), bd(pslice('kt', si, pr))], axis=0), contract,
        preferred_element_type=F32))
    g_strict = stage(lambda si, pr: jnp.where(strict[fwd_of[si, pr]], gmat[si, pr][:CHUNK], 0.0))
    bdv = stage(lambda si, pr: bd(pslice('v', si, pr)))
    akv = stage(lambda si, pr: mm(g_strict[si, pr][:, LANES:], bdv[si, pr]))
    tmat = stage(lambda si, pr: eye + g_strict[si, pr][:, :LANES])
    pmat = stage(lambda si, pr: mm(g_strict[si, pr][:, :LANES], bd(g_strict[si, pr][:, :LANES])))
    for _ in range(4):
        res = stage(lambda si, pr: mm(pmat[si, pr], jnp.concatenate([bd(tmat[si, pr]), bd(pmat[si, pr])], axis=1)))
        tmat = stage(lambda si, pr: tmat[si, pr] + res[si, pr][:, :LANES])
        pmat = stage(lambda si, pr: res[si, pr][:, LANES:])
    tmat = stage(lambda si, pr: tmat[si, pr] + mm(pmat[si, pr], bd(tmat[si, pr])))
    wu = stage(lambda si, pr: mm(tmat[si, pr], jnp.concatenate([bd(pslice('at', si, pr)), bd(akv[si, pr])], axis=1)))
    rows_p = lax.broadcasted_iota(jnp.int32, (LANES, LANES), 0)
    lanes_p = lax.broadcasted_iota(jnp.int32, (LANES, LANES), 1)
    same_head = (rows_p // HEAD_DIM) == (lanes_p // HEAD_DIM)
    diag_p = rows_p == lanes_p
    s_old = stage(lambda si, pr: s_ref[si, pr].astype(BF16))
    su = stage(lambda si, pr: mm(jnp.concatenate([wu[si, pr][:, :LANES], pslice('rt', si, pr)], axis=0),
                                 s_old[si, pr]))
    umat = stage(lambda si, pr: su[si, pr][:CHUNK] + wu[si, pr][:, LANES:])
    y_out = stage(lambda si, pr: su[si, pr][CHUNK:] + mm(
        jnp.where(incl[fwd_of[si, pr]], gmat[si, pr][CHUNK:], 0.0),
        jnp.concatenate([bd(umat[si, pr]), bdv[si, pr]], axis=0)))

    def next_state(si, pr):
        bk_t = jnp.concatenate([pslice('bh', si, pr), pslice('kh', si, pr)], axis=0).T
        lhs = jnp.concatenate([bk_t, jnp.where(diag_p, pslice('gam', si, pr), 0.0)], axis=1)
        rhs = jnp.concatenate([umat[si, pr].astype(BF16), pslice('v', si, pr).astype(BF16), s_old[si, pr]], axis=0)
        return jnp.where(same_head, mm(lhs, rhs), 0.0)

    new_state = stage(next_state)

    for j in range(bb):
        for pr in range(RWKV_HEADS // 2):
            s_ref[j, pr] = new_state[j, pr]
            yf_ref[j, :, LANES * pr:LANES * (pr + 1)] = y_out[j, pr]

    @pl.when(c < ncr)
    def _():
        for j in range(bb):
            for pr in range(RWKV_HEADS // 2):
                s_ref[bb + j, pr] = new_state[bb + j, pr]
                yb_ref[j, :, LANES * pr:LANES * (pr + 1)] = y_out[bb + j, pr]


def _scan(real, meta, batch, seq, bb):
    ncr = seq // CHUNK
    assert batch % bb == 0

    def fidx(c):
        return jnp.maximum(c - 1, 0)

    def bidx(c):
        return jnp.maximum(ncr - 1 - c, 0)

    def specs(idx, d):
        one = pl.BlockSpec((bb, CHUNK, RWKV_WIDTH), lambda g, c: (g, idx(c), 0))
        two = pl.BlockSpec((1, bb, CHUNK, RWKV_WIDTH), lambda g, c: (d, g, idx(c), 0))
        return [one, one, one, two, two, two]

    mone = pl.BlockSpec((bb, N_META, RWKV_WIDTH), lambda g, c: (g, 0, 0))
    mtwo = pl.BlockSpec((1, bb, N_META, RWKV_WIDTH), lambda g, c: (0, g, 0, 0))
    y_shape = jax.ShapeDtypeStruct((batch, seq, RWKV_WIDTH), F32)
    return pl.pallas_call(
        functools.partial(_scan_kernel, ncr, bb),
        grid=(batch // bb, ncr + 1),
        in_specs=specs(fidx, 0) + specs(bidx, 1) + [mone, mone, mone, mtwo, mtwo, mtwo],
        out_specs=(pl.BlockSpec((bb, CHUNK, RWKV_WIDTH), lambda g, c: (g, fidx(c), 0)),
                   pl.BlockSpec((bb, CHUNK, RWKV_WIDTH), lambda g, c: (g, bidx(c), 0))),
        out_shape=(y_shape, y_shape),
        scratch_shapes=[pltpu.VMEM((2 * bb, RWKV_HEADS // 2, LANES, LANES), F32)],
        compiler_params=pltpu.CompilerParams(
            dimension_semantics=("parallel", "arbitrary"), vmem_limit_bytes=VMEM_LIMIT),
        name="wkv_scan",
    )(*real, *real, *meta)


def _attn_kernel(nkb, tk, group, shift_ref, q_ref, k_ref, vt_ref, km_ref, vmt_ref, lam_ref, sg_ref, o_ref):
    q = q_ref[0]
    tq = q.shape[0]
    lane = lax.broadcasted_iota(jnp.int32, (tq, LANES), 1)
    zero = jnp.zeros_like(q)
    qc = (jnp.where(lane < DIFF_QK_DIM, q, zero), jnp.where(lane >= DIFF_QK_DIM, q, zero))
    contract = (((1,), (1,)), ((), ()))
    shift = shift_ref[0]

    def blocks(online, kbs, vtbs, state):
        scores = [[lax.dot_general(kb, qc[comp], contract, preferred_element_type=F32) for comp in range(2)]
                  for kb in kbs]
        state = list(state)
        for s_pair, vtb in zip(scores, vtbs):
            for comp in range(2):
                m, acc = state[comp]
                s = s_pair[comp]
                if online:
                    m_new = jnp.maximum(m, jnp.max(s, axis=0, keepdims=True))
                    pexp = jnp.exp2(s - m_new).astype(BF16)
                    acc = jnp.exp2(m - m_new) * acc
                    m = m_new
                else:
                    pexp = jnp.exp2(s - shift).astype(BF16)
                state[comp] = (m, acc + jnp.dot(vtb, pexp, preferred_element_type=F32))
        return tuple(state)

    def group_at(i):
        starts = [pl.multiple_of((i * group + j) * tk, tk) for j in range(group)]
        return ([k_ref[0, pl.ds(st_j, tk), :] for st_j in starts],
                [vt_ref[0, :, pl.ds(st_j, tk)] for st_j in starts])

    def attend(online):
        init = tuple((jnp.full((1, tq), -jnp.inf, F32), jnp.zeros((VT_ROWS, tq), F32)) for _ in range(2))
        kbs0, vtbs0 = group_at(0)
        state = blocks(online, [km_ref[0]] + kbs0, [vmt_ref[0]] + vtbs0, init)
        state = lax.fori_loop(1, nkb // group, lambda i, st: blocks(online, *group_at(i), st), state)
        lam_v = lam_ref[...]
        lam = (jnp.exp(jnp.sum(lam_v[0:1] * lam_v[1:2], axis=-1, keepdims=True))
               - jnp.exp(jnp.sum(lam_v[2:3] * lam_v[3:4], axis=-1, keepdims=True)) + LAMBDA_INIT)
        (_, acc0), (_, acc1) = state
        a0, l0 = acc0[:DIFF_V_DIM], acc0[DIFF_V_DIM:DIFF_V_DIM + 1]
        a1, l1 = acc1[:DIFF_V_DIM], acc1[DIFF_V_DIM:DIFF_V_DIM + 1]
        o = a0 / l0 - lam * (a1 / l1)
        o = o * lax.rsqrt(jnp.mean(o * o, axis=0, keepdims=True) + NORM_EPS) * sg_ref[...]
        o_ref[0] = (o * (1.0 - LAMBDA_INIT)).T

    @pl.when(shift <= MAX_CONST_SHIFT)
    def _():
        attend(online=False)

    @pl.when(shift > MAX_CONST_SHIFT)
    def _():
        attend(online=True)


def _attention(q, k, vt, km, vmt, shift, lam, subln, tiles):
    batch, seq, _ = q.shape
    tq, tk = tiles.tq, tiles.tk
    nkb = seq // tk
    return pl.pallas_call(
        functools.partial(_attn_kernel, nkb, tk, math.gcd(nkb, tiles.attn_group)),
        grid=(batch, DIFF_HEADS, seq // tq),
        in_specs=[
            pl.BlockSpec(memory_space=pltpu.SMEM),
            pl.BlockSpec((1, tq, LANES), lambda b, h, i: (b, i, h)),
            pl.BlockSpec((1, seq, LANES), lambda b, h, i: (b, 0, h)),
            pl.BlockSpec((1, VT_ROWS, seq), lambda b, h, i: (b, h, 0)),
            pl.BlockSpec((1, N_META, LANES), lambda b, h, i: (b, 0, h)),
            pl.BlockSpec((1, VT_ROWS, N_META), lambda b, h, i: (b, h, 0)),
            _const_spec((4, DIFF_QK_DIM)),
            _const_spec((DIFF_V_DIM, 1)),
        ],
        out_specs=pl.BlockSpec((1, tq, LANES), lambda b, h, i: (b, i, h)),
        out_shape=jax.ShapeDtypeStruct((batch, seq, DIFF_WIDTH), F32),
        compiler_params=pltpu.CompilerParams(
            dimension_semantics=("parallel", "parallel", "parallel"), vmem_limit_bytes=VMEM_LIMIT),
        name="diff_attn",
    )(shift, q, k, vt, km, vmt, lam, subln)


def _post_kernel(x_ref, yf_ref, ybw_ref, bonus_ref, g_ref, yb_ref, lng_ref, lnb_ref, seg_ref, wout_ref, ln2_ref,
                 wg_ref, wu_ref, wd_ref, o_ref):
    ys = yf_ref[...] + ybw_ref[...]
    inv_n = 1.0 / HEAD_DIM
    mu = _seg_sum(ys, seg_ref) * inv_n
    dev = ys - mu
    var = _seg_sum(dev * dev, seg_ref) * inv_n
    yn = dev * lax.rsqrt(var + GN_EPS) * lng_ref[...] + lnb_ref[...]
    ya = (yn + bonus_ref[...]) * g_ref[...]
    mix = jnp.concatenate([ya, yb_ref[...]], axis=1).astype(BF16)
    x1 = x_ref[...] + jnp.dot(mix, wout_ref[...], preferred_element_type=F32)
    n2 = x1 * lax.rsqrt(jnp.mean(x1 * x1, axis=-1, keepdims=True) + NORM_EPS) * ln2_ref[...]
    n2 = n2.astype(BF16)
    gate = jnp.dot(n2, wg_ref[...], preferred_element_type=F32)
    up = jnp.dot(n2, wu_ref[...], preferred_element_type=F32)
    hid = (gate * jax.nn.sigmoid(gate) * up).astype(BF16)
    o_ref[...] = x1 + jnp.dot(hid, wd_ref[...], preferred_element_type=F32)


def _post(x2, y_f, y_b, bonus, g, yb, weights, tt):
    rows = x2.shape[0]
    row512 = pl.BlockSpec((tt, RWKV_WIDTH), lambda i: (i, 0))
    in_specs = [
        pl.BlockSpec((tt, D_MODEL), lambda i: (i, 0)),
        row512, row512, row512, row512, row512,
        _const_spec((1, RWKV_WIDTH)), _const_spec((1, RWKV_WIDTH)), _const_spec((256, 256)),
        _const_spec((D_MODEL, D_MODEL)), _const_spec((1, D_MODEL)),
        _const_spec((D_MODEL, D_FF)), _const_spec((D_MODEL, D_FF)), _const_spec((D_FF, D_MODEL)),
    ]
    return pl.pallas_call(
        _post_kernel,
        grid=(rows // tt,),
        in_specs=in_specs,
        out_specs=pl.BlockSpec((tt, D_MODEL), lambda i: (i, 0)),
        out_shape=jax.ShapeDtypeStruct((rows, D_MODEL), F32),
        compiler_params=pltpu.CompilerParams(dimension_semantics=("parallel",), vmem_limit_bytes=VMEM_LIMIT),
        name="post",
    )(x2, y_f, y_b, bonus, g, yb, *weights)


def _rope_tables(total):
    freqs = ROPE_THETA ** (-jnp.arange(0, ROPE_DIMS, 2, dtype=F32) / ROPE_DIMS)
    ang = jnp.arange(total, dtype=F32)[:, None] * freqs[None, :]
    cos, sin = jnp.cos(ang), jnp.sin(ang)
    ones = jnp.ones((total, DIFF_QK_DIM - ROPE_DIMS), F32)
    zeros = jnp.zeros((total, ROPE_HALF), F32)
    zrest = jnp.zeros((total, DIFF_QK_DIM - ROPE_DIMS), F32)
    cos64 = jnp.concatenate([cos, cos, ones], axis=1)
    sa64 = jnp.concatenate([-sin, zeros, zrest], axis=1)
    sb64 = jnp.concatenate([zeros, sin, zrest], axis=1)
    return tuple(jnp.concatenate([t, t], axis=1) for t in (cos64, sa64, sb64))


def _prepare_weights(p):
    def lora_cols(scale_fn):
        cols = [scale_fn(p['mix_w'][0])[:, None] * p['decay_w1'][0],
                scale_fn(p['mix_w'][1])[:, None] * p['decay_w1'][1],
                scale_fn(p['mix_a'][0])[:, None] * p['aaa_a1'][0],
                scale_fn(p['mix_a'][1])[:, None] * p['aaa_a1'][1],
                scale_fn(p['mix_g'])[:, None] * p['gate_g1'],
                jnp.zeros((D_MODEL, GATE_LORA_PAD - GATE_LORA), F32)]
        return jnp.concatenate(cols, axis=1)

    wcat = jnp.concatenate([p['w_in'], lora_cols(lambda m: 1.0 - m), lora_cols(lambda m: 0.5 * m)], axis=1)

    def blockdiag(w):
        z = jnp.zeros_like(w[0])
        return jnp.concatenate([jnp.concatenate([w[0], z], axis=1), jnp.concatenate([z, w[1]], axis=1)], axis=0)

    g2 = jnp.concatenate([p['gate_g2'], jnp.zeros((GATE_LORA_PAD - GATE_LORA, RWKV_WIDTH), F32)], axis=0)
    seg = (jnp.arange(256)[:, None] // HEAD_DIM == jnp.arange(256)[None, :] // HEAD_DIM).astype(BF16)
    pre_w = (
        p['ln1_g'][None, :], wcat.astype(BF16), p['conv_rkv'],
        p['decay_w0'].reshape(1, 2 * RWKV_WIDTH), blockdiag(p['decay_w2']).astype(BF16),
        p['aaa_a0'].reshape(1, 2 * RWKV_WIDTH), blockdiag(p['aaa_a2']).astype(BF16),
        g2.astype(BF16), p['k_k'][None, :], p['k_a'][None, :], p['r_k'].reshape(1, RWKV_WIDTH),
        jnp.tile(p['q_norm_g'], DIFF_WIDTH // DIFF_QK_DIM)[None, :],
        jnp.tile(p['k_norm_g'], DIFF_WIDTH // DIFF_QK_DIM)[None, :], seg,
    )
    post_w = (
        p['lnx_g'][None, :], p['lnx_b'][None, :], seg, p['w_out'].astype(BF16), p['ln2_g'][None, :],
        p['w_gate'].astype(BF16), p['w_up'].astype(BF16), p['w_down'].astype(BF16),
    )
    return pre_w, post_w


def _score_bound(q_gain, k_gain):
    q_max = math.sqrt(DIFF_QK_DIM) * jnp.max(jnp.abs(q_gain)) * (DIFF_QK_DIM ** -0.5 * LOG2E)
    k_max = math.sqrt(DIFF_QK_DIM) * jnp.max(jnp.abs(k_gain))
    return (q_max * k_max).reshape(1).astype(F32)


def _tile_rows(seq, cap):
    tt = min(seq, cap)
    assert seq % tt == 0 and tt % CHUNK == 0
    return tt


class Tiles(NamedTuple):
    pre: int
    post: int
    tq: int
    tk: int
    attn_group: int
    scan_bb: int


def _tiles(batch, seq):
    row_tile = _tile_rows(seq, ROW_TILE)
    long_seq = seq >= LONG_SEQ
    return Tiles(pre=row_tile, post=row_tile,
                 tq=_tile_rows(seq, 256 if long_seq else 1024), tk=_tile_rows(seq, 256),
                 attn_group=16 if long_seq else 8, scan_bb=math.gcd(batch, SCAN_ROWS))


def _trunk(x, meta, pre_w, post_w, shift, lam, subln, tiles):
    batch, seq, _ = x.shape
    rows = batch * seq
    x2 = x.reshape(rows, D_MODEL)
    tabs = _rope_tables(N_META + seq)
    tabs_meta = tuple(t[:N_META] for t in tabs)
    tabs_real = tuple(t[N_META:] for t in tabs)

    real = _pre_real(x2, meta, tabs_real, pre_w, seq, tiles.pre)
    metao = _pre_meta(x, meta, tabs_meta, pre_w)
    (r, v, kk, lw, kd, b, g, bonus, q, k, vd) = real
    (mr, mv, mkk, mlw, mkd, mb, _, _, _, mk, mvd) = metao

    def r3(a, n):
        return a.reshape(a.shape[:-2] + (batch, n, a.shape[-1]))

    y_f, y_b = _scan((r3(r, seq), r3(v, seq), r3(kk, seq), r3(lw, seq), r3(kd, seq), r3(b, seq)),
                     (r3(mr, N_META), r3(mv, N_META), r3(mkk, N_META), r3(mlw, N_META), r3(mkd, N_META),
                      r3(mb, N_META)), batch, seq, tiles.scan_bb)
    yb = _attention(r3(q, seq), r3(k, seq), vd, r3(mk, N_META), mvd, shift, lam, subln, tiles)
    out = _post(x2, y_f.reshape(rows, RWKV_WIDTH), y_b.reshape(rows, RWKV_WIDTH), bonus, g,
                yb.reshape(rows, DIFF_WIDTH), post_w, tiles.post)
    return out.reshape(batch, seq, D_MODEL)


def kernel(x_prompt, x_sample, meta_tokens, ln1_g, w_in, conv_rkv, mix_w, mix_a, mix_g, decay_w0, decay_w1,
           decay_w2, aaa_a0, aaa_a1, aaa_a2, gate_g1, gate_g2, k_k, k_a, r_k, lnx_g, lnx_b, q_norm_g, k_norm_g,
           diff_lambdas, subln_g, w_out, ln2_g, w_gate, w_up, w_down):
    params = dict(ln1_g=ln1_g, w_in=w_in, conv_rkv=conv_rkv, mix_w=mix_w, mix_a=mix_a, mix_g=mix_g,
                  decay_w0=decay_w0, decay_w1=decay_w1, decay_w2=decay_w2, aaa_a0=aaa_a0, aaa_a1=aaa_a1,
                  aaa_a2=aaa_a2, gate_g1=gate_g1, gate_g2=gate_g2, k_k=k_k, k_a=k_a, r_k=r_k, lnx_g=lnx_g,
                  lnx_b=lnx_b, q_norm_g=q_norm_g, k_norm_g=k_norm_g, w_out=w_out, ln2_g=ln2_g, w_gate=w_gate,
                  w_up=w_up, w_down=w_down)
    p = {name: arr[0] for name, arr in params.items()}
    pre_w, post_w = _prepare_weights(p)
    lam = diff_lambdas[0]
    shift = _score_bound(q_norm_g[0], k_norm_g[0])
    subln = subln_g[0][:, None]
    outs = []
    for x in (x_prompt, x_sample):
        outs.append(_trunk(x, meta_tokens, pre_w, post_w, shift, lam, subln, _tiles(x.shape[0], x.shape[1])))
    return tuple(outs)
```

```python
import functools
import math
from typing import NamedTuple

import jax
import jax.numpy as jnp
from jax import lax
from jax.experimental import pallas as pl
from jax.experimental.pallas import tpu as pltpu

F32 = jnp.float32
BF16 = jnp.bfloat16

D_MODEL = 1024
N_META = 16
RWKV_HEADS = 8
HEAD_DIM = 64
RWKV_WIDTH = 512
DIFF_HEADS = 4
DIFF_QK_DIM = 64
DIFF_V_DIM = 128
DIFF_WIDTH = 512
IN_WIDTH = 3072
DECAY_LORA = 64
AAA_LORA = 64
GATE_LORA = 160
GATE_LORA_PAD = 256
LORA_BLOCK = 512
CAT_WIDTH = IN_WIDTH + 2 * LORA_BLOCK
ROPE_THETA = 500000.0
ROPE_DIMS = 16
ROPE_HALF = 8
D_FF = 2816
NORM_EPS = 1e-6
GN_EPS = 64e-5
KK_NORM_FLOOR = 1e-12
DECAY_SCALE = math.exp(-0.5)
LAMBDA_INIT = 0.8 - 0.6 * math.exp(0.0)

LANES = 128
SUBLANES = 8
HALO = SUBLANES
CHUNK = 64
MAX_CONST_SHIFT = 60.0
ROW_TILE = 512
LONG_SEQ = 8192
SCAN_ROWS = 4
VT_ROWS = DIFF_V_DIM + 16
LOG2E = math.log2(math.e)
VMEM_LIMIT = 56 * 1024 * 1024


def _const_spec(shape):
    zeros = (0,) * len(shape)
    return pl.BlockSpec(shape, lambda *_: zeros, pipeline_mode=pl.Buffered(1))


def _seg_sum(x, seg_ref):
    seg = seg_ref[...]
    parts = []
    for c in range(x.shape[1] // 256):
        xs = x[:, 256 * c:256 * (c + 1)].astype(BF16)
        parts.append(jnp.dot(xs, seg, preferred_element_type=F32))
    return jnp.concatenate(parts, axis=1)


def _roll_lanes(x, shift):
    parts = [pltpu.roll(x[:, LANES * c:LANES * (c + 1)], shift, 1) for c in range(x.shape[1] // LANES)]
    return jnp.concatenate(parts, axis=1)


def _pre_compute(xx, tt, tabs, w, outs):
    (cos_ref, sa_ref, sb_ref) = tabs
    (ln1_ref, wcat_ref, conv_ref, w0_ref, w2_ref, a0_ref, a2_ref, g2_ref, kk_ref, ka_ref, rk_ref,
     qg_ref, kg_ref, seg_ref) = w
    (o_r, o_v, o_kk, o_lw, o_kd, o_b, o_g, o_bonus, o_q, o_k, o_vd) = outs
    rows = tt + 2 * HALO

    ms = jnp.mean(xx * xx, axis=-1, keepdims=True)
    n = xx * lax.rsqrt(ms + NORM_EPS) * ln1_ref[...]
    p = jnp.dot(n.astype(BF16), wcat_ref[...], preferred_element_type=F32)

    def prev_rows(a):
        return pltpu.roll(a, 1, 0)[HALO:HALO + tt]

    def next_rows(a):
        return pltpu.roll(a, rows - 1, 0)[HALO:HALO + tt]

    c0 = 3 * RWKV_WIDTH
    rkv_in = p[:, :c0]
    conv = conv_ref[...]
    rkv = conv[0:1] * prev_rows(rkv_in) + conv[1:2] * rkv_in[HALO:HALO + tt] + conv[2:3] * next_rows(rkv_in)
    r = rkv[:, :RWKV_WIDTH]
    k = rkv[:, RWKV_WIDTH:2 * RWKV_WIDTH]
    v = rkv[:, 2 * RWKV_WIDTH:]

    l_self = p[HALO:HALO + tt, IN_WIDTH:IN_WIDTH + LORA_BLOCK]
    l_nbr = p[:, IN_WIDTH + LORA_BLOCK:]
    lora = l_self + prev_rows(l_nbr) + next_rows(l_nbr)

    tw = jnp.tanh(lora[:, 0:128]).astype(BF16)
    zw = jnp.dot(tw, w2_ref[...], preferred_element_type=F32) + w0_ref[...]
    lw = -DECAY_SCALE * jax.nn.sigmoid(zw)
    za = jnp.dot(lora[:, 128:256].astype(BF16), a2_ref[...], preferred_element_type=F32) + a0_ref[...]
    aa = jax.nn.sigmoid(za)
    sg = jax.nn.sigmoid(lora[:, 256:512]).astype(BF16)
    g = jnp.dot(sg, g2_ref[...], preferred_element_type=F32)

    kk = k * kk_ref[...]
    kk = kk * lax.rsqrt(jnp.maximum(_seg_sum(kk * kk, seg_ref), KK_NORM_FLOOR ** 2))
    k_a = ka_ref[...]
    kd0 = k * (1.0 + (aa[:, :RWKV_WIDTH] - 1.0) * k_a)
    kd1 = k * (1.0 + (aa[:, RWKV_WIDTH:] - 1.0) * k_a)
    bonus = _seg_sum(r * rk_ref[...] * (kd0 + kd1), seg_ref) * v

    o_r[...] = r
    o_v[...] = v
    o_kk[...] = kk
    o_lw[0] = lw[:, :RWKV_WIDTH]
    o_lw[1] = lw[:, RWKV_WIDTH:]
    o_kd[0] = kd0
    o_kd[1] = kd1
    o_b[0] = kk * aa[:, :RWKV_WIDTH]
    o_b[1] = kk * aa[:, RWKV_WIDTH:]
    o_g[...] = g
    o_bonus[...] = bonus

    cos_t = jnp.concatenate([cos_ref[...]] * 4, axis=1)
    sa_t = jnp.concatenate([sa_ref[...]] * 4, axis=1)
    sb_t = jnp.concatenate([sb_ref[...]] * 4, axis=1)

    def qk_norm_rope(xq, g_ref):
        ssq = _seg_sum(xq * xq, seg_ref) * (1.0 / DIFF_QK_DIM)
        xn = xq * lax.rsqrt(ssq + NORM_EPS) * g_ref[...]
        return xn * cos_t + _roll_lanes(xn, LANES - ROPE_HALF) * sa_t + _roll_lanes(xn, ROPE_HALF) * sb_t

    qd = p[HALO:HALO + tt, c0:c0 + DIFF_WIDTH]
    kd = p[HALO:HALO + tt, c0 + DIFF_WIDTH:c0 + 2 * DIFF_WIDTH]
    vd = p[HALO:HALO + tt, c0 + 2 * DIFF_WIDTH:c0 + 3 * DIFF_WIDTH]
    o_q[...] = (qk_norm_rope(qd, qg_ref) * (DIFF_QK_DIM ** -0.5 * LOG2E)).astype(BF16)
    o_k[...] = qk_norm_rope(kd, kg_ref).astype(BF16)
    vd_t = vd.T
    fill = (lax.broadcasted_iota(jnp.int32, (VT_ROWS - DIFF_V_DIM, tt), 0) == 0).astype(F32)
    o_vd[0] = jnp.concatenate(
        [piece for h in range(DIFF_HEADS) for piece in (vd_t[DIFF_V_DIM * h:DIFF_V_DIM * (h + 1)], fill)],
        axis=0).astype(BF16)


def _pre_real_kernel(nt, tt, xm_ref, xp_ref, xn_ref, meta_ref, cos_ref, sa_ref, sb_ref, *rest):
    w, outs = rest[:14], rest[14:]
    j = pl.program_id(0) % nt
    xp = jnp.where(j == 0, meta_ref[N_META - HALO:N_META, :], xp_ref[...])
    xn = jnp.where(j == nt - 1, 0.0, xn_ref[...])
    xx = jnp.concatenate([xp, xm_ref[...], xn], axis=0)
    _pre_compute(xx, tt, (cos_ref, sa_ref, sb_ref), w, outs)


def _pre_meta_kernel(xn_ref, meta_ref, cos_ref, sa_ref, sb_ref, *rest):
    w, outs = rest[:14], rest[14:]
    xx = jnp.concatenate([jnp.zeros((HALO, D_MODEL), F32), meta_ref[...], xn_ref[0]], axis=0)
    _pre_compute(xx, N_META, (cos_ref, sa_ref, sb_ref), w, outs)


def _pre_out_shapes(batch, seq):
    rows = batch * seq
    one = jax.ShapeDtypeStruct((rows, RWKV_WIDTH), F32)
    two = jax.ShapeDtypeStruct((2, rows, RWKV_WIDTH), F32)
    half = jax.ShapeDtypeStruct((rows, DIFF_WIDTH), BF16)
    v_t = jax.ShapeDtypeStruct((batch, DIFF_HEADS * VT_ROWS, seq), BF16)
    return (one, one, one, two, two, two, one, one, half, half, v_t)


def _pre_out_specs(tt, nt):
    one = pl.BlockSpec((tt, RWKV_WIDTH), lambda i: (i, 0))
    two = pl.BlockSpec((2, tt, RWKV_WIDTH), lambda i: (0, i, 0))
    v_t = pl.BlockSpec((1, DIFF_HEADS * VT_ROWS, tt), lambda i: (i // nt, 0, i % nt))
    return (one, one, one, two, two, two, one, one, one, one, v_t)


def _pre_weight_specs():
    shapes = [(1, D_MODEL), (D_MODEL, CAT_WIDTH), (3, 3 * RWKV_WIDTH), (1, 2 * RWKV_WIDTH),
              (2 * DECAY_LORA, 2 * RWKV_WIDTH), (1, 2 * RWKV_WIDTH), (2 * AAA_LORA, 2 * RWKV_WIDTH),
              (GATE_LORA_PAD, RWKV_WIDTH), (1, RWKV_WIDTH), (1, RWKV_WIDTH), (1, RWKV_WIDTH),
              (1, DIFF_WIDTH), (1, DIFF_WIDTH), (256, 256)]
    return [_const_spec(s) for s in shapes]


def _pre_real(x2, meta, tabs_real, weights, seq, tt):
    rows = x2.shape[0]
    nt = seq // tt
    hb = tt // HALO
    last_halo = rows // HALO - 1
    in_specs = [
        pl.BlockSpec((tt, D_MODEL), lambda i: (i, 0)),
        pl.BlockSpec((HALO, D_MODEL), lambda i: (jnp.maximum(i * hb - 1, 0), 0)),
        pl.BlockSpec((HALO, D_MODEL), lambda i: (jnp.minimum((i + 1) * hb, last_halo), 0)),
        _const_spec((N_META, D_MODEL)),
    ] + [pl.BlockSpec((tt, LANES), lambda i: (i % nt, 0))] * 3 + _pre_weight_specs()
    return pl.pallas_call(
        functools.partial(_pre_real_kernel, nt, tt),
        grid=(rows // tt,),
        in_specs=in_specs,
        out_specs=_pre_out_specs(tt, nt),
        out_shape=_pre_out_shapes(rows // seq, seq),
        compiler_params=pltpu.CompilerParams(dimension_semantics=("parallel",), vmem_limit_bytes=VMEM_LIMIT),
        name="pre_real",
    )(x2, x2, x2, meta, *tabs_real, *weights)


def _pre_meta(x3, meta, tabs_meta, weights):
    batch = x3.shape[0]
    in_specs = [
        pl.BlockSpec((1, HALO, D_MODEL), lambda b: (b, 0, 0)),
        _const_spec((N_META, D_MODEL)),
    ] + [_const_spec((N_META, LANES))] * 3 + _pre_weight_specs()
    return pl.pallas_call(
        _pre_meta_kernel,
        grid=(batch,),
        in_specs=in_specs,
        out_specs=_pre_out_specs(N_META, 1),
        out_shape=_pre_out_shapes(batch, N_META),
        compiler_params=pltpu.CompilerParams(dimension_semantics=("parallel",), vmem_limit_bytes=VMEM_LIMIT),
        name="pre_meta",
    )(x3, meta, *tabs_meta, *weights)


def _scan_kernel(ncr, bb, *refs):
    fwd_refs, bwd_refs, meta_refs = refs[0:6], refs[6:12], refs[12:18]
    yf_ref, yb_ref, s_ref = refs[18:21]
    c = pl.program_id(1)
    is_meta = c == 0

    @pl.when(c == 0)
    def _():
        s_ref[...] = jnp.zeros_like(s_ref)

    pad = jnp.zeros((CHUNK - N_META, RWKV_WIDTH), F32)
    row = lax.broadcasted_iota(jnp.int32, (CHUNK, CHUNK), 0)
    colc = lax.broadcasted_iota(jnp.int32, (CHUNK, CHUNK), 1)
    m_incl = {True: (colc <= row).astype(BF16), False: (colc >= row).astype(BF16)}

    def load(refs6, j, fwd):
        vals = []
        for idx, ref in enumerate(refs6):
            x = ref[j] if idx < 3 else ref[0, j]
            if fwd:
                mref = meta_refs[idx]
                meta = mref[j] if idx < 3 else mref[0, j]
                x = jnp.where(is_meta, jnp.concatenate([meta, pad], axis=0), x)
            vals.append(x)
        return vals

    streams = [(load(fwd_refs, j, True), True) for j in range(bb)]
    streams += [(load(bwd_refs, j, False), False) for j in range(bb)]

    prep = []
    for (r, v, kk, lw, kd, b), fwd in streams:
        lw_hi = lw.astype(BF16)
        lw_lo = (lw - lw_hi.astype(F32)).astype(BF16)
        ci = (jnp.dot(m_incl[fwd], lw_hi, preferred_element_type=F32)
              + jnp.dot(m_incl[fwd], lw_lo, preferred_element_type=F32))
        tot = jnp.sum(lw, axis=0, keepdims=True)
        e_inv = jnp.exp(-ci)
        e_rest = jnp.exp(tot - ci)
        prep.append(dict(fwd=fwd, gam=jnp.exp(tot), rt=r * jnp.exp(ci), at=-kk * jnp.exp(ci - lw), kt=kd * e_inv,
                         bt=b * e_inv, kh=kd * e_rest, bh=b * e_rest, v=v))

    rowp = lax.broadcasted_iota(jnp.int32, (CHUNK, LANES), 0)
    lanep = lax.broadcasted_iota(jnp.int32, (CHUNK, LANES), 1)
    left = lanep < HEAD_DIM
    eye = (lanep % CHUNK == rowp).astype(F32)
    row2 = lax.broadcasted_iota(jnp.int32, (CHUNK, 2 * LANES), 0)
    src2 = lax.broadcasted_iota(jnp.int32, (CHUNK, 2 * LANES), 1) % CHUNK
    strict = {True: src2 < row2, False: src2 > row2}
    incl = {True: src2 <= row2, False: src2 >= row2}

    def bd(x):
        return jnp.concatenate([jnp.where(left, x, 0.0), jnp.where(left, 0.0, x)], axis=0).astype(BF16)

    chains = [(si, pr) for si in range(len(prep)) for pr in range(RWKV_HEADS // 2)]

    def pslice(name, si, pr):
        return prep[si][name][:, LANES * pr:LANES * (pr + 1)]

    def stage(fn):
        return {ch: fn(*ch) for ch in chains}

    def mm(a, b):
        return jnp.dot(a.astype(BF16), b, preferred_element_type=F32)

    fwd_of = {ch: prep[ch[0]]['fwd'] for ch in chains}
    contract = (((1,), (1,)), ((), ()))
    gmat = stage(lambda si, pr: lax.dot_general(
        jnp.concatenate([pslice('at', si, pr), pslice('rt', si, pr)], axis=0).astype(BF16),
        jnp.concatenate([bd(pslice('bt', si, pr)), bd(pslice('kt', si, pr))], axis=0), contract,
        preferred_element_type=F32))
    g_strict = stage(lambda si, pr: jnp.where(strict[fwd_of[si, pr]], gmat[si, pr][:CHUNK], 0.0))
    bdv = stage(lambda si, pr: bd(pslice('v', si, pr)))
    akv = stage(lambda si, pr: mm(g_strict[si, pr][:, LANES:], bdv[si, pr]))
    tmat = stage(lambda si, pr: eye + g_strict[si, pr][:, :LANES])
    pmat = stage(lambda si, pr: mm(g_strict[si, pr][:, :LANES], bd(g_strict[si, pr][:, :LANES])))
    for _ in range(4):
        res = stage(lambda si, pr: mm(pmat[si, pr], jnp.concatenate([bd(tmat[si, pr]), bd(pmat[si, pr])], axis=1)))
        tmat = stage(lambda si, pr: tmat[si, pr] + res[si, pr][:, :LANES])
        pmat = stage(lambda si, pr: res[si, pr][:, LANES:])
    tmat = stage(lambda si, pr: tmat[si, pr] + mm(pmat[si, pr], bd(tmat[si, pr])))
    wu = stage(lambda si, pr: mm(tmat[si, pr], jnp.concatenate([bd(pslice('at', si, pr)), bd(akv[si, pr])], axis=1)))
    rows_p = lax.broadcasted_iota(jnp.int32, (LANES, LANES), 0)
    lanes_p = lax.broadcasted_iota(jnp.int32, (LANES, LANES), 1)
    same_head = (rows_p // HEAD_DIM) == (lanes_p // HEAD_DIM)
    diag_p = rows_p == lanes_p
    s_old = stage(lambda si, pr: s_ref[si, pr].astype(BF16))
    su = stage(lambda si, pr: mm(jnp.concatenate([wu[si, pr][:, :LANES], pslice('rt', si, pr)], axis=0),
                                 s_old[si, pr]))
    umat = stage(lambda si, pr: su[si, pr][:CHUNK] + wu[si, pr][:, LANES:])
    y_out = stage(lambda si, pr: su[si, pr][CHUNK:] + mm(
        jnp.where(incl[fwd_of[si, pr]], gmat[si, pr][CHUNK:], 0.0),
        jnp.concatenate([bd(umat[si, pr]), bdv[si, pr]], axis=0)))

    def next_state(si, pr):
        bk_t = jnp.concatenate([pslice('bh', si, pr), pslice('kh', si, pr)], axis=0).T
        lhs = jnp.concatenate([bk_t, jnp.where(diag_p, pslice('gam', si, pr), 0.0)], axis=1)
        rhs = jnp.concatenate([umat[si, pr].astype(BF16), pslice('v', si, pr).astype(BF16), s_old[si, pr]], axis=0)
        return jnp.where(same_head, mm(lhs, rhs), 0.0)

    new_state = stage(next_state)

    for j in range(bb):
        for pr in range(RWKV_HEADS // 2):
            s_ref[j, pr] = new_state[j, pr]
            yf_ref[j, :, LANES * pr:LANES * (pr + 1)] = y_out[j, pr]

    @pl.when(c < ncr)
    def _():
        for j in range(bb):
            for pr in range(RWKV_HEADS // 2):
                s_ref[bb + j, pr] = new_state[bb + j, pr]
                yb_ref[j, :, LANES * pr:LANES * (pr + 1)] = y_out[bb + j, pr]


def _scan(real, meta, batch, seq, bb):
    ncr = seq // CHUNK
    assert batch % bb == 0

    def fidx(c):
        return jnp.maximum(c - 1, 0)

    def bidx(c):
        return jnp.maximum(ncr - 1 - c, 0)

    def specs(idx, d):
        one = pl.BlockSpec((bb, CHUNK, RWKV_WIDTH), lambda g, c: (g, idx(c), 0))
        two = pl.BlockSpec((1, bb, CHUNK, RWKV_WIDTH), lambda g, c: (d, g, idx(c), 0))
        return [one, one, one, two, two, two]

    mone = pl.BlockSpec((bb, N_META, RWKV_WIDTH), lambda g, c: (g, 0, 0))
    mtwo = pl.BlockSpec((1, bb, N_META, RWKV_WIDTH), lambda g, c: (0, g, 0, 0))
    y_shape = jax.ShapeDtypeStruct((batch, seq, RWKV_WIDTH), F32)
    return pl.pallas_call(
        functools.partial(_scan_kernel, ncr, bb),
        grid=(batch // bb, ncr + 1),
        in_specs=specs(fidx, 0) + specs(bidx, 1) + [mone, mone, mone, mtwo, mtwo, mtwo],
        out_specs=(pl.BlockSpec((bb, CHUNK, RWKV_WIDTH), lambda g, c: (g, fidx(c), 0)),
                   pl.BlockSpec((bb, CHUNK, RWKV_WIDTH), lambda g, c: (g, bidx(c), 0))),
        out_shape=(y_shape, y_shape),
        scratch_shapes=[pltpu.VMEM((2 * bb, RWKV_HEADS // 2, LANES, LANES), F32)],
        compiler_params=pltpu.CompilerParams(
            dimension_semantics=("parallel", "arbitrary"), vmem_limit_bytes=VMEM_LIMIT),
        name="wkv_scan",
    )(*real, *real, *meta)


def _attn_kernel(nkb, tk, group, shift_ref, q_ref, k_ref, vt_ref, km_ref, vmt_ref, lam_ref, sg_ref, o_ref):
    q = q_ref[0]
    tq = q.shape[0]
    lane = lax.broadcasted_iota(jnp.int32, (tq, LANES), 1)
    zero = jnp.zeros_like(q)
    qc = (jnp.where(lane < DIFF_QK_DIM, q, zero), jnp.where(lane >= DIFF_QK_DIM, q, zero))
    contract = (((1,), (1,)), ((), ()))
    shift = shift_ref[0]

    def blocks(online, kbs, vtbs, state):
        scores = [[lax.dot_general(kb, qc[comp], contract, preferred_element_type=F32) for comp in range(2)]
                  for kb in kbs]
        state = list(state)
        for s_pair, vtb in zip(scores, vtbs):
            for comp in range(2):
                m, acc = state[comp]
                s = s_pair[comp]
                if online:
                    m_new = jnp.maximum(m, jnp.max(s, axis=0, keepdims=True))
                    pexp = jnp.exp2(s - m_new).astype(BF16)
                    acc = jnp.exp2(m - m_new) * acc
                    m = m_new
                else:
                    pexp = jnp.exp2(s - shift).astype(BF16)
                state[comp] = (m, acc + jnp.dot(vtb, pexp, preferred_element_type=F32))
        return tuple(state)

    def group_at(i):
        starts = [pl.multiple_of((i * group + j) * tk, tk) for j in range(group)]
        return ([k_ref[0, pl.ds(st_j, tk), :] for st_j in starts],
                [vt_ref[0, :, pl.ds(st_j, tk)] for st_j in starts])

    def attend(online):
        init = tuple((jnp.full((1, tq), -jnp.inf, F32), jnp.zeros((VT_ROWS, tq), F32)) for _ in range(2))
        kbs0, vtbs0 = group_at(0)
        state = blocks(online, [km_ref[0]] + kbs0, [vmt_ref[0]] + vtbs0, init)
        state = lax.fori_loop(1, nkb // group, lambda i, st: blocks(online, *group_at(i), st), state)
        lam_v = lam_ref[...]
        lam = (jnp.exp(jnp.sum(lam_v[0:1] * lam_v[1:2], axis=-1, keepdims=True))
               - jnp.exp(jnp.sum(lam_v[2:3] * lam_v[3:4], axis=-1, keepdims=True)) + LAMBDA_INIT)
        (_, acc0), (_, acc1) = state
        a0, l0 = acc0[:DIFF_V_DIM], acc0[DIFF_V_DIM:DIFF_V_DIM + 1]
        a1, l1 = acc1[:DIFF_V_DIM], acc1[DIFF_V_DIM:DIFF_V_DIM + 1]
        o = a0 / l0 - lam * (a1 / l1)
        o = o * lax.rsqrt(jnp.mean(o * o, axis=0, keepdims=True) + NORM_EPS) * sg_ref[...]
        o_ref[0] = (o * (1.0 - LAMBDA_INIT)).T

    @pl.when(shift <= MAX_CONST_SHIFT)
    def _():
        attend(online=False)

    @pl.when(shift > MAX_CONST_SHIFT)
    def _():
        attend(online=True)


def _attention(q, k, vt, km, vmt, shift, lam, subln, tiles):
    batch, seq, _ = q.shape
    tq, tk = tiles.tq, tiles.tk
    nkb = seq // tk
    return pl.pallas_call(
        functools.partial(_attn_kernel, nkb, tk, math.gcd(nkb, tiles.attn_group)),
        grid=(batch, DIFF_HEADS, seq // tq),
        in_specs=[
            pl.BlockSpec(memory_space=pltpu.SMEM),
            pl.BlockSpec((1, tq, LANES), lambda b, h, i: (b, i, h)),
            pl.BlockSpec((1, seq, LANES), lambda b, h, i: (b, 0, h)),
            pl.BlockSpec((1, VT_ROWS, seq), lambda b, h, i: (b, h, 0)),
            pl.BlockSpec((1, N_META, LANES), lambda b, h, i: (b, 0, h)),
            pl.BlockSpec((1, VT_ROWS, N_META), lambda b, h, i: (b, h, 0)),
            _const_spec((4, DIFF_QK_DIM)),
            _const_spec((DIFF_V_DIM, 1)),
        ],
        out_specs=pl.BlockSpec((1, tq, LANES), lambda b, h, i: (b, i, h)),
        out_shape=jax.ShapeDtypeStruct((batch, seq, DIFF_WIDTH), F32),
        compiler_params=pltpu.CompilerParams(
            dimension_semantics=("parallel", "parallel", "parallel"), vmem_limit_bytes=VMEM_LIMIT),
        name="diff_attn",
    )(shift, q, k, vt, km, vmt, lam, subln)


def _post_kernel(x_ref, yf_ref, ybw_ref, bonus_ref, g_ref, yb_ref, lng_ref, lnb_ref, seg_ref, wout_ref, ln2_ref,
                 wg_ref, wu_ref, wd_ref, o_ref):
    ys = yf_ref[...] + ybw_ref[...]
    inv_n = 1.0 / HEAD_DIM
    mu = _seg_sum(ys, seg_ref) * inv_n
    dev = ys - mu
    var = _seg_sum(dev * dev, seg_ref) * inv_n
    yn = dev * lax.rsqrt(var + GN_EPS) * lng_ref[...] + lnb_ref[...]
    ya = (yn + bonus_ref[...]) * g_ref[...]
    mix = jnp.concatenate([ya, yb_ref[...]], axis=1).astype(BF16)
    x1 = x_ref[...] + jnp.dot(mix, wout_ref[...], preferred_element_type=F32)
    n2 = x1 * lax.rsqrt(jnp.mean(x1 * x1, axis=-1, keepdims=True) + NORM_EPS) * ln2_ref[...]
    n2 = n2.astype(BF16)
    gate = jnp.dot(n2, wg_ref[...], preferred_element_type=F32)
    up = jnp.dot(n2, wu_ref[...], preferred_element_type=F32)
    hid = (gate * jax.nn.sigmoid(gate) * up).astype(BF16)
    o_ref[...] = x1 + jnp.dot(hid, wd_ref[...], preferred_element_type=F32)


def _post(x2, y_f, y_b, bonus, g, yb, weights, tt):
    rows = x2.shape[0]
    row512 = pl.BlockSpec((tt, RWKV_WIDTH), lambda i: (i, 0))
    in_specs = [
        pl.BlockSpec((tt, D_MODEL), lambda i: (i, 0)),
        row512, row512, row512, row512, row512,
        _const_spec((1, RWKV_WIDTH)), _const_spec((1, RWKV_WIDTH)), _const_spec((256, 256)),
        _const_spec((D_MODEL, D_MODEL)), _const_spec((1, D_MODEL)),
        _const_spec((D_MODEL, D_FF)), _const_spec((D_MODEL, D_FF)), _const_spec((D_FF, D_MODEL)),
    ]
    return pl.pallas_call(
        _post_kernel,
        grid=(rows // tt,),
        in_specs=in_specs,
        out_specs=pl.BlockSpec((tt, D_MODEL), lambda i: (i, 0)),
        out_shape=jax.ShapeDtypeStruct((rows, D_MODEL), F32),
        compiler_params=pltpu.CompilerParams(dimension_semantics=("parallel",), vmem_limit_bytes=VMEM_LIMIT),
        name="post",
    )(x2, y_f, y_b, bonus, g, yb, *weights)


def _rope_tables(total):
    freqs = ROPE_THETA ** (-jnp.arange(0, ROPE_DIMS, 2, dtype=F32) / ROPE_DIMS)
    ang = jnp.arange(total, dtype=F32)[:, None] * freqs[None, :]
    cos, sin = jnp.cos(ang), jnp.sin(ang)
    ones = jnp.ones((total, DIFF_QK_DIM - ROPE_DIMS), F32)
    zeros = jnp.zeros((total, ROPE_HALF), F32)
    zrest = jnp.zeros((total, DIFF_QK_DIM - ROPE_DIMS), F32)
    cos64 = jnp.concatenate([cos, cos, ones], axis=1)
    sa64 = jnp.concatenate([-sin, zeros, zrest], axis=1)
    sb64 = jnp.concatenate([zeros, sin, zrest], axis=1)
    return tuple(jnp.concatenate([t, t], axis=1) for t in (cos64, sa64, sb64))


def _prepare_weights(p):
    def lora_cols(scale_fn):
        cols = [scale_fn(p['mix_w'][0])[:, None] * p['decay_w1'][0],
                scale_fn(p['mix_w'][1])[:, None] * p['decay_w1'][1],
                scale_fn(p['mix_a'][0])[:, None] * p['aaa_a1'][0],
                scale_fn(p['mix_a'][1])[:, None] * p['aaa_a1'][1],
                scale_fn(p['mix_g'])[:, None] * p['gate_g1'],
                jnp.zeros((D_MODEL, GATE_LORA_PAD - GATE_LORA), F32)]
        return jnp.concatenate(cols, axis=1)

    wcat = jnp.concatenate([p['w_in'], lora_cols(lambda m: 1.0 - m), lora_cols(lambda m: 0.5 * m)], axis=1)

    def blockdiag(w):
        z = jnp.zeros_like(w[0])
        return jnp.concatenate([jnp.concatenate([w[0], z], axis=1), jnp.concatenate([z, w[1]], axis=1)], axis=0)

    g2 = jnp.concatenate([p['gate_g2'], jnp.zeros((GATE_LORA_PAD - GATE_LORA, RWKV_WIDTH), F32)], axis=0)
    seg = (jnp.arange(256)[:, None] // HEAD_DIM == jnp.arange(256)[None, :] // HEAD_DIM).astype(BF16)
    pre_w = (
        p['ln1_g'][None, :], wcat.astype(BF16), p['conv_rkv'],
        p['decay_w0'].reshape(1, 2 * RWKV_WIDTH), blockdiag(p['decay_w2']).astype(BF16),
        p['aaa_a0'].reshape(1, 2 * RWKV_WIDTH), blockdiag(p['aaa_a2']).astype(BF16),
        g2.astype(BF16), p['k_k'][None, :], p['k_a'][None, :], p['r_k'].reshape(1, RWKV_WIDTH),
        jnp.tile(p['q_norm_g'], DIFF_WIDTH // DIFF_QK_DIM)[None, :],
        jnp.tile(p['k_norm_g'], DIFF_WIDTH // DIFF_QK_DIM)[None, :], seg,
    )
    post_w = (
        p['lnx_g'][None, :], p['lnx_b'][None, :], seg, p['w_out'].astype(BF16), p['ln2_g'][None, :],
        p['w_gate'].astype(BF16), p['w_up'].astype(BF16), p['w_down'].astype(BF16),
    )
    return pre_w, post_w


def _score_bound(q_gain, k_gain):
    q_max = math.sqrt(DIFF_QK_DIM) * jnp.max(jnp.abs(q_gain)) * (DIFF_QK_DIM ** -0.5 * LOG2E)
    k_max = math.sqrt(DIFF_QK_DIM) * jnp.max(jnp.abs(k_gain))
    return (q_max * k_max).reshape(1).astype(F32)


def _tile_rows(seq, cap):
    tt = min(seq, cap)
    assert seq % tt == 0 and tt % CHUNK == 0
    return tt


class Tiles(NamedTuple):
    pre: int
    post: int
    tq: int
    tk: int
    attn_group: int
    scan_bb: int


def _tiles(batch, seq):
    row_tile = _tile_rows(seq, ROW_TILE)
    long_seq = seq >= LONG_SEQ
    return Tiles(pre=row_tile, post=row_tile,
                 tq=_tile_rows(seq, 512 if long_seq else 1024), tk=_tile_rows(seq, 256),
                 attn_group=16 if long_seq else 8, scan_bb=math.gcd(batch, SCAN_ROWS))


def _trunk(x, meta, pre_w, post_w, shift, lam, subln, tiles):
    batch, seq, _ = x.shape
    rows = batch * seq
    x2 = x.reshape(rows, D_MODEL)
    tabs = _rope_tables(N_META + seq)
    tabs_meta = tuple(t[:N_META] for t in tabs)
    tabs_real = tuple(t[N_META:] for t in tabs)

    real = _pre_real(x2, meta, tabs_real, pre_w, seq, tiles.pre)
    metao = _pre_meta(x, meta, tabs_meta, pre_w)
    (r, v, kk, lw, kd, b, g, bonus, q, k, vd) = real
    (mr, mv, mkk, mlw, mkd, mb, _, _, _, mk, mvd) = metao

    def r3(a, n):
        return a.reshape(a.shape[:-2] + (batch, n, a.shape[-1]))

    y_f, y_b = _scan((r3(r, seq), r3(v, seq), r3(kk, seq), r3(lw, seq), r3(kd, seq), r3(b, seq)),
                     (r3(mr, N_META), r3(mv, N_META), r3(mkk, N_META), r3(mlw, N_META), r3(mkd, N_META),
                      r3(mb, N_META)), batch, seq, tiles.scan_bb)
    yb = _attention(r3(q, seq), r3(k, seq), vd, r3(mk, N_META), mvd, shift, lam, subln, tiles)
    out = _post(x2, y_f.reshape(rows, RWKV_WIDTH), y_b.reshape(rows, RWKV_WIDTH), bonus, g,
                yb.reshape(rows, DIFF_WIDTH), post_w, tiles.post)
    return out.reshape(batch, seq, D_MODEL)


def kernel(x_prompt, x_sample, meta_tokens, ln1_g, w_in, conv_rkv, mix_w, mix_a, mix_g, decay_w0, decay_w1,
           decay_w2, aaa_a0, aaa_a1, aaa_a2, gate_g1, gate_g2, k_k, k_a, r_k, lnx_g, lnx_b, q_norm_g, k_norm_g,
           diff_lambdas, subln_g, w_out, ln2_g, w_gate, w_up, w_down):
    params = dict(ln1_g=ln1_g, w_in=w_in, conv_rkv=conv_rkv, mix_w=mix_w, mix_a=mix_a, mix_g=mix_g,
                  decay_w0=decay_w0, decay_w1=decay_w1, decay_w2=decay_w2, aaa_a0=aaa_a0, aaa_a1=aaa_a1,
                  aaa_a2=aaa_a2, gate_g1=gate_g1, gate_g2=gate_g2, k_k=k_k, k_a=k_a, r_k=r_k, lnx_g=lnx_g,
                  lnx_b=lnx_b, q_norm_g=q_norm_g, k_norm_g=k_norm_g, w_out=w_out, ln2_g=ln2_g, w_gate=w_gate,
                  w_up=w_up, w_down=w_down)
    p = {name: arr[0] for name, arr in params.items()}
    pre_w, post_w = _prepare_weights(p)
    lam = diff_lambdas[0]
    shift = _score_bound(q_norm_g[0], k_norm_g[0])
    subln = subln_g[0][:, None]
    outs = []
    for x in (x_prompt, x_sample):
        outs.append(_trunk(x, meta_tokens, pre_w, post_w, shift, lam, subln, _tiles(x.shape[0], x.shape[1])))
    return tuple(outs)
```

```python
import functools
import math
from typing import NamedTuple

import jax
import jax.numpy as jnp
from jax import lax
from jax.experimental import pallas as pl
from jax.experimental.pallas import tpu as pltpu

F32 = jnp.float32
BF16 = jnp.bfloat16

D_MODEL = 1024
N_META = 16
RWKV_HEADS = 8
HEAD_DIM = 64
RWKV_WIDTH = 512
DIFF_HEADS = 4
DIFF_QK_DIM = 64
DIFF_V_DIM = 128
DIFF_WIDTH = 512
IN_WIDTH = 3072
DECAY_LORA = 64
AAA_LORA = 64
GATE_LORA = 160
GATE_LORA_PAD = 256
LORA_BLOCK = 512
CAT_WIDTH = IN_WIDTH + 2 * LORA_BLOCK
ROPE_THETA = 500000.0
ROPE_DIMS = 16
ROPE_HALF = 8
D_FF = 2816
NORM_EPS = 1e-6
GN_EPS = 64e-5
KK_NORM_FLOOR = 1e-12
DECAY_SCALE = math.exp(-0.5)
LAMBDA_INIT = 0.8 - 0.6 * math.exp(0.0)

LANES = 128
SUBLANES = 8
HALO = SUBLANES
CHUNK = 64
META_SLOT = 32
META_SLOTS_PER_TILE = 8
MAX_CONST_SHIFT = 60.0
ROW_TILE = 512
LONG_SEQ = 8192
SCAN_ROWS = 4
VT_ROWS = DIFF_V_DIM + 16
LOG2E = math.log2(math.e)
VMEM_LIMIT = 56 * 1024 * 1024


def _const_spec(shape):
    zeros = (0,) * len(shape)
    return pl.BlockSpec(shape, lambda *_: zeros, pipeline_mode=pl.Buffered(1))


def _seg_sum(x, seg_ref):
    seg = seg_ref[...]
    parts = []
    for c in range(x.shape[1] // 256):
        xs = x[:, 256 * c:256 * (c + 1)].astype(BF16)
        parts.append(jnp.dot(xs, seg, preferred_element_type=F32))
    return jnp.concatenate(parts, axis=1)


def _roll_lanes(x, shift):
    parts = [pltpu.roll(x[:, LANES * c:LANES * (c + 1)], shift, 1) for c in range(x.shape[1] // LANES)]
    return jnp.concatenate(parts, axis=1)


def _pre_compute(xx, tt, tabs, w, outs):
    (cos_ref, sa_ref, sb_ref) = tabs
    (ln1_ref, wcat_ref, conv_ref, w0_ref, w2_ref, a0_ref, a2_ref, g2_ref, kk_ref, ka_ref, rk_ref,
     qg_ref, kg_ref, seg_ref) = w
    (o_r, o_v, o_kk, o_lw, o_kd, o_b, o_g, o_bonus, o_q, o_k, o_vd) = outs
    rows = tt + 2 * HALO

    ms = jnp.mean(xx * xx, axis=-1, keepdims=True)
    n = xx * lax.rsqrt(ms + NORM_EPS) * ln1_ref[...]
    p = jnp.dot(n.astype(BF16), wcat_ref[...], preferred_element_type=F32)

    def prev_rows(a):
        return pltpu.roll(a, 1, 0)[HALO:HALO + tt]

    def next_rows(a):
        return pltpu.roll(a, rows - 1, 0)[HALO:HALO + tt]

    c0 = 3 * RWKV_WIDTH
    rkv_in = p[:, :c0]
    conv = conv_ref[...]
    rkv = conv[0:1] * prev_rows(rkv_in) + conv[1:2] * rkv_in[HALO:HALO + tt] + conv[2:3] * next_rows(rkv_in)
    r = rkv[:, :RWKV_WIDTH]
    k = rkv[:, RWKV_WIDTH:2 * RWKV_WIDTH]
    v = rkv[:, 2 * RWKV_WIDTH:]

    l_self = p[HALO:HALO + tt, IN_WIDTH:IN_WIDTH + LORA_BLOCK]
    l_nbr = p[:, IN_WIDTH + LORA_BLOCK:]
    lora = l_self + prev_rows(l_nbr) + next_rows(l_nbr)

    tw = jnp.tanh(lora[:, 0:128]).astype(BF16)
    zw = jnp.dot(tw, w2_ref[...], preferred_element_type=F32) + w0_ref[...]
    lw = -DECAY_SCALE * jax.nn.sigmoid(zw)
    za = jnp.dot(lora[:, 128:256].astype(BF16), a2_ref[...], preferred_element_type=F32) + a0_ref[...]
    aa = jax.nn.sigmoid(za)
    sg = jax.nn.sigmoid(lora[:, 256:512]).astype(BF16)
    g = jnp.dot(sg, g2_ref[...], preferred_element_type=F32)

    kk = k * kk_ref[...]
    kk = kk * lax.rsqrt(jnp.maximum(_seg_sum(kk * kk, seg_ref), KK_NORM_FLOOR ** 2))
    k_a = ka_ref[...]
    kd0 = k * (1.0 + (aa[:, :RWKV_WIDTH] - 1.0) * k_a)
    kd1 = k * (1.0 + (aa[:, RWKV_WIDTH:] - 1.0) * k_a)
    bonus = _seg_sum(r * rk_ref[...] * (kd0 + kd1), seg_ref) * v

    o_r[...] = r
    o_v[...] = v
    o_kk[...] = kk
    o_lw[0] = lw[:, :RWKV_WIDTH]
    o_lw[1] = lw[:, RWKV_WIDTH:]
    o_kd[0] = kd0
    o_kd[1] = kd1
    o_b[0] = kk * aa[:, :RWKV_WIDTH]
    o_b[1] = kk * aa[:, RWKV_WIDTH:]
    o_g[...] = g
    o_bonus[...] = bonus

    cos_t = jnp.concatenate([cos_ref[...]] * 4, axis=1)
    sa_t = jnp.concatenate([sa_ref[...]] * 4, axis=1)
    sb_t = jnp.concatenate([sb_ref[...]] * 4, axis=1)

    def qk_norm_rope(xq, g_ref):
        ssq = _seg_sum(xq * xq, seg_ref) * (1.0 / DIFF_QK_DIM)
        xn = xq * lax.rsqrt(ssq + NORM_EPS) * g_ref[...]
        return xn * cos_t + _roll_lanes(xn, LANES - ROPE_HALF) * sa_t + _roll_lanes(xn, ROPE_HALF) * sb_t

    qd = p[HALO:HALO + tt, c0:c0 + DIFF_WIDTH]
    kd = p[HALO:HALO + tt, c0 + DIFF_WIDTH:c0 + 2 * DIFF_WIDTH]
    vd = p[HALO:HALO + tt, c0 + 2 * DIFF_WIDTH:c0 + 3 * DIFF_WIDTH]
    o_q[...] = (qk_norm_rope(qd, qg_ref) * (DIFF_QK_DIM ** -0.5 * LOG2E)).astype(BF16)
    o_k[...] = qk_norm_rope(kd, kg_ref).astype(BF16)
    vd_t = vd.T
    fill = (lax.broadcasted_iota(jnp.int32, (VT_ROWS - DIFF_V_DIM, tt), 0) == 0).astype(F32)
    o_vd[0] = jnp.concatenate(
        [piece for h in range(DIFF_HEADS) for piece in (vd_t[DIFF_V_DIM * h:DIFF_V_DIM * (h + 1)], fill)],
        axis=0).astype(BF16)


def _pre_real_kernel(nt, tt, xm_ref, xp_ref, xn_ref, meta_ref, cos_ref, sa_ref, sb_ref, *rest):
    w, outs = rest[:14], rest[14:]
    j = pl.program_id(0) % nt
    xp = jnp.where(j == 0, meta_ref[N_META - HALO:N_META, :], xp_ref[...])
    xn = jnp.where(j == nt - 1, 0.0, xn_ref[...])
    xx = jnp.concatenate([xp, xm_ref[...], xn], axis=0)
    _pre_compute(xx, tt, (cos_ref, sa_ref, sb_ref), w, outs)


def _pre_meta_kernel(tm, xm_ref, cos_ref, sa_ref, sb_ref, *rest):
    w, outs = rest[:14], rest[14:]
    halo = jnp.zeros((HALO, D_MODEL), F32)
    _pre_compute(jnp.concatenate([halo, xm_ref[...], halo], axis=0), tm, (cos_ref, sa_ref, sb_ref), w, outs)


def _pre_out_shapes(batch, seq):
    rows = batch * seq
    one = jax.ShapeDtypeStruct((rows, RWKV_WIDTH), F32)
    two = jax.ShapeDtypeStruct((2, rows, RWKV_WIDTH), F32)
    half = jax.ShapeDtypeStruct((rows, DIFF_WIDTH), BF16)
    v_t = jax.ShapeDtypeStruct((batch, DIFF_HEADS * VT_ROWS, seq), BF16)
    return (one, one, one, two, two, two, one, one, half, half, v_t)


def _pre_out_specs(tt, nt):
    one = pl.BlockSpec((tt, RWKV_WIDTH), lambda i: (i, 0))
    two = pl.BlockSpec((2, tt, RWKV_WIDTH), lambda i: (0, i, 0))
    v_t = pl.BlockSpec((1, DIFF_HEADS * VT_ROWS, tt), lambda i: (i // nt, 0, i % nt))
    return (one, one, one, two, two, two, one, one, one, one, v_t)


def _pre_weight_specs():
    shapes = [(1, D_MODEL), (D_MODEL, CAT_WIDTH), (3, 3 * RWKV_WIDTH), (1, 2 * RWKV_WIDTH),
              (2 * DECAY_LORA, 2 * RWKV_WIDTH), (1, 2 * RWKV_WIDTH), (2 * AAA_LORA, 2 * RWKV_WIDTH),
              (GATE_LORA_PAD, RWKV_WIDTH), (1, RWKV_WIDTH), (1, RWKV_WIDTH), (1, RWKV_WIDTH),
              (1, DIFF_WIDTH), (1, DIFF_WIDTH), (256, 256)]
    return [_const_spec(s) for s in shapes]


def _pre_real(x2, meta, tabs_real, weights, seq, tt):
    rows = x2.shape[0]
    nt = seq // tt
    hb = tt // HALO
    last_halo = rows // HALO - 1
    in_specs = [
        pl.BlockSpec((tt, D_MODEL), lambda i: (i, 0)),
        pl.BlockSpec((HALO, D_MODEL), lambda i: (jnp.maximum(i * hb - 1, 0), 0)),
        pl.BlockSpec((HALO, D_MODEL), lambda i: (jnp.minimum((i + 1) * hb, last_halo), 0)),
        _const_spec((N_META, D_MODEL)),
    ] + [pl.BlockSpec((tt, LANES), lambda i: (i % nt, 0))] * 3 + _pre_weight_specs()
    return pl.pallas_call(
        functools.partial(_pre_real_kernel, nt, tt),
        grid=(rows // tt,),
        in_specs=in_specs,
        out_specs=_pre_out_specs(tt, nt),
        out_shape=_pre_out_shapes(rows // seq, seq),
        compiler_params=pltpu.CompilerParams(dimension_semantics=("parallel",), vmem_limit_bytes=VMEM_LIMIT),
        name="pre_real",
    )(x2, x2, x2, meta, *tabs_real, *weights)


def _pre_meta(x3, meta, tabs, weights):
    batch = x3.shape[0]
    group = math.gcd(batch, META_SLOTS_PER_TILE)
    tm = group * META_SLOT
    steps = batch // group
    slots = jnp.concatenate(
        [jnp.broadcast_to(meta[None], (batch, N_META, D_MODEL)), x3[:, :HALO],
         jnp.zeros((batch, META_SLOT - N_META - HALO, D_MODEL), F32)], axis=1).reshape(batch * META_SLOT, D_MODEL)
    tabs_tile = tuple(jnp.tile(t[:META_SLOT], (group, 1)) for t in tabs)
    in_specs = ([pl.BlockSpec((tm, D_MODEL), lambda i: (i, 0))] + [_const_spec((tm, LANES))] * 3
                + _pre_weight_specs())
    outs = pl.pallas_call(
        functools.partial(_pre_meta_kernel, tm),
        grid=(steps,),
        in_specs=in_specs,
        out_specs=_pre_out_specs(tm, 1),
        out_shape=_pre_out_shapes(steps, tm),
        compiler_params=pltpu.CompilerParams(dimension_semantics=("parallel",), vmem_limit_bytes=VMEM_LIMIT),
        name="pre_meta",
    )(slots, *tabs_tile, *weights)

    def meta_rows(a):
        return a.reshape(a.shape[:-2] + (batch, META_SLOT, a.shape[-1]))[..., :N_META, :]

    v_t = outs[-1].reshape(steps, DIFF_HEADS * VT_ROWS, group, META_SLOT)[..., :N_META]
    v_t = v_t.transpose(0, 2, 1, 3).reshape(batch, DIFF_HEADS * VT_ROWS, N_META)
    return tuple(meta_rows(a) for a in outs[:-1]) + (v_t,)


def _scan_kernel(ncr, bb, *refs):
    fwd_refs, bwd_refs, meta_refs = refs[0:6], refs[6:12], refs[12:18]
    yf_ref, yb_ref, s_ref = refs[18:21]
    c = pl.program_id(1)
    is_meta = c == 0

    @pl.when(c == 0)
    def _():
        s_ref[...] = jnp.zeros_like(s_ref)

    pad = jnp.zeros((CHUNK - N_META, RWKV_WIDTH), F32)
    row_w = lax.broadcasted_iota(jnp.int32, (CHUNK, RWKV_WIDTH), 0)

    def cumsum_rows(x, fwd):
        step = 1
        while step < CHUNK:
            if fwd:
                x = x + jnp.where(row_w >= step, pltpu.roll(x, step, 0), 0.0)
            else:
                x = x + jnp.where(row_w < CHUNK - step, pltpu.roll(x, CHUNK - step, 0), 0.0)
            step *= 2
        return x

    def load(refs6, j, fwd):
        vals = []
        for idx, ref in enumerate(refs6):
            x = ref[j] if idx < 3 else ref[0, j]
            if fwd:
                mref = meta_refs[idx]
                meta = mref[j] if idx < 3 else mref[0, j]
                x = jnp.where(is_meta, jnp.concatenate([meta, pad], axis=0), x)
            vals.append(x)
        return vals

    streams = [(load(fwd_refs, j, True), True) for j in range(bb)]
    streams += [(load(bwd_refs, j, False), False) for j in range(bb)]

    prep = []
    for (r, v, kk, lw, kd, b), fwd in streams:
        ci = cumsum_rows(lw, fwd)
        tot = jnp.sum(lw, axis=0, keepdims=True)
        e_inv = jnp.exp(-ci)
        e_rest = jnp.exp(tot - ci)
        prep.append(dict(fwd=fwd, gam=jnp.exp(tot), rt=r * jnp.exp(ci), at=-kk * jnp.exp(ci - lw), kt=kd * e_inv,
                         bt=b * e_inv, kh=kd * e_rest, bh=b * e_rest, v=v))

    rowp = lax.broadcasted_iota(jnp.int32, (CHUNK, LANES), 0)
    lanep = lax.broadcasted_iota(jnp.int32, (CHUNK, LANES), 1)
    left = lanep < HEAD_DIM
    eye = (lanep % CHUNK == rowp).astype(F32)
    row2 = lax.broadcasted_iota(jnp.int32, (CHUNK, 2 * LANES), 0)
    src2 = lax.broadcasted_iota(jnp.int32, (CHUNK, 2 * LANES), 1) % CHUNK
    strict = {True: src2 < row2, False: src2 > row2}
    incl = {True: src2 <= row2, False: src2 >= row2}

    def bd(x):
        return jnp.concatenate([jnp.where(left, x, 0.0), jnp.where(left, 0.0, x)], axis=0).astype(BF16)

    chains = [(si, pr) for si in range(len(prep)) for pr in range(RWKV_HEADS // 2)]

    def pslice(name, si, pr):
        return prep[si][name][:, LANES * pr:LANES * (pr + 1)]

    def stage(fn):
        return {ch: fn(*ch) for ch in chains}

    def mm(a, b):
        return jnp.dot(a.astype(BF16), b, preferred_element_type=F32)

    fwd_of = {ch: prep[ch[0]]['fwd'] for ch in chains}
    contract = (((1,), (1,)), ((), ()))
    gmat = stage(lambda si, pr: lax.dot_general(
        jnp.concatenate([pslice('at', si, pr), pslice('rt', si, pr)], axis=0).astype(BF16),
        jnp.concatenate([bd(pslice('bt', si, pr)), bd(pslice('kt', si, pr))], axis=0), contract,
        preferred_element_type=F32))
    g_strict = stage(lambda si, pr: jnp.where(strict[fwd_of[si, pr]], gmat[si, pr][:CHUNK], 0.0))
    bdv = stage(lambda si, pr: bd(pslice('v', si, pr)))
    akv = stage(lambda si, pr: mm(g_strict[si, pr][:, LANES:], bdv[si, pr]))
    tmat = stage(lambda si, pr: eye + g_strict[si, pr][:, :LANES])
    pmat = stage(lambda si, pr: mm(g_strict[si, pr][:, :LANES], bd(g_strict[si, pr][:, :LANES])))
    for _ in range(4):
        res = stage(lambda si, pr: mm(pmat[si, pr], jnp.concatenate([bd(tmat[si, pr]), bd(pmat[si, pr])], axis=1)))
        tmat = stage(lambda si, pr: tmat[si, pr] + res[si, pr][:, :LANES])
        pmat = stage(lambda si, pr: res[si, pr][:, LANES:])
    tmat = stage(lambda si, pr: tmat[si, pr] + mm(pmat[si, pr], bd(tmat[si, pr])))
    wu = stage(lambda si, pr: mm(tmat[si, pr], jnp.concatenate([bd(pslice('at', si, pr)), bd(akv[si, pr])], axis=1)))
    rows_p = lax.broadcasted_iota(jnp.int32, (LANES, LANES), 0)
    lanes_p = lax.broadcasted_iota(jnp.int32, (LANES, LANES), 1)
    same_head = (rows_p // HEAD_DIM) == (lanes_p // HEAD_DIM)
    diag_p = rows_p == lanes_p
    s_old = stage(lambda si, pr: s_ref[si, pr].astype(BF16))
    su = stage(lambda si, pr: mm(jnp.concatenate([wu[si, pr][:, :LANES], pslice('rt', si, pr)], axis=0),
                                 s_old[si, pr]))
    umat = stage(lambda si, pr: su[si, pr][:CHUNK] + wu[si, pr][:, LANES:])
    y_out = stage(lambda si, pr: su[si, pr][CHUNK:] + mm(
        jnp.where(incl[fwd_of[si, pr]], gmat[si, pr][CHUNK:], 0.0),
        jnp.concatenate([bd(umat[si, pr]), bdv[si, pr]], axis=0)))

    def next_state(si, pr):
        bk_t = jnp.concatenate([pslice('bh', si, pr), pslice('kh', si, pr)], axis=0).T
        lhs = jnp.concatenate([bk_t, jnp.where(diag_p, pslice('gam', si, pr), 0.0)], axis=1)
        rhs = jnp.concatenate([umat[si, pr].astype(BF16), pslice('v', si, pr).astype(BF16), s_old[si, pr]], axis=0)
        return jnp.where(same_head, mm(lhs, rhs), 0.0)

    new_state = stage(next_state)

    for j in range(bb):
        for pr in range(RWKV_HEADS // 2):
            s_ref[j, pr] = new_state[j, pr]
            yf_ref[j, :, LANES * pr:LANES * (pr + 1)] = y_out[j, pr]

    @pl.when(c < ncr)
    def _():
        for j in range(bb):
            for pr in range(RWKV_HEADS // 2):
                s_ref[bb + j, pr] = new_state[bb + j, pr]
                yb_ref[j, :, LANES * pr:LANES * (pr + 1)] = y_out[bb + j, pr]


def _scan(real, meta, batch, seq, bb):
    ncr = seq // CHUNK
    assert batch % bb == 0

    def fidx(c):
        return jnp.maximum(c - 1, 0)

    def bidx(c):
        return jnp.maximum(ncr - 1 - c, 0)

    def specs(idx, d):
        one = pl.BlockSpec((bb, CHUNK, RWKV_WIDTH), lambda g, c: (g, idx(c), 0))
        two = pl.BlockSpec((1, bb, CHUNK, RWKV_WIDTH), lambda g, c: (d, g, idx(c), 0))
        return [one, one, one, two, two, two]

    mone = pl.BlockSpec((bb, N_META, RWKV_WIDTH), lambda g, c: (g, 0, 0))
    mtwo = pl.BlockSpec((1, bb, N_META, RWKV_WIDTH), lambda g, c: (0, g, 0, 0))
    y_shape = jax.ShapeDtypeStruct((batch, seq, RWKV_WIDTH), F32)
    return pl.pallas_call(
        functools.partial(_scan_kernel, ncr, bb),
        grid=(batch // bb, ncr + 1),
        in_specs=specs(fidx, 0) + specs(bidx, 1) + [mone, mone, mone, mtwo, mtwo, mtwo],
        out_specs=(pl.BlockSpec((bb, CHUNK, RWKV_WIDTH), lambda g, c: (g, fidx(c), 0)),
                   pl.BlockSpec((bb, CHUNK, RWKV_WIDTH), lambda g, c: (g, bidx(c), 0))),
        out_shape=(y_shape, y_shape),
        scratch_shapes=[pltpu.VMEM((2 * bb, RWKV_HEADS // 2, LANES, LANES), F32)],
        compiler_params=pltpu.CompilerParams(
            dimension_semantics=("parallel", "arbitrary"), vmem_limit_bytes=VMEM_LIMIT),
        name="wkv_scan",
    )(*real, *real, *meta)


def _attn_kernel(nkb, tk, group, shift_ref, q_ref, k_ref, vt_ref, km_ref, vmt_ref, lam_ref, sg_ref, o_ref):
    q = q_ref[0]
    tq = q.shape[0]
    lane = lax.broadcasted_iota(jnp.int32, (tq, LANES), 1)
    zero = jnp.zeros_like(q)
    qc = (jnp.where(lane < DIFF_QK_DIM, q, zero), jnp.where(lane >= DIFF_QK_DIM, q, zero))
    contract = (((1,), (1,)), ((), ()))
    shift = shift_ref[0]

    def blocks(online, kbs, vtbs, state):
        scores = [[lax.dot_general(kb, qc[comp], contract, preferred_element_type=F32) for comp in range(2)]
                  for kb in kbs]
        state = list(state)
        for s_pair, vtb in zip(scores, vtbs):
            for comp in range(2):
                m, acc = state[comp]
                s = s_pair[comp]
                if online:
                    m_new = jnp.maximum(m, jnp.max(s, axis=0, keepdims=True))
                    pexp = jnp.exp2(s - m_new).astype(BF16)
                    acc = jnp.exp2(m - m_new) * acc
                    m = m_new
                else:
                    pexp = jnp.exp2(s - shift).astype(BF16)
                state[comp] = (m, acc + jnp.dot(vtb, pexp, preferred_element_type=F32))
        return tuple(state)

    def group_at(i):
        starts = [pl.multiple_of((i * group + j) * tk, tk) for j in range(group)]
        return ([k_ref[0, pl.ds(st_j, tk), :] for st_j in starts],
                [vt_ref[0, :, pl.ds(st_j, tk)] for st_j in starts])

    def attend(online):
        init = tuple((jnp.full((1, tq), -jnp.inf, F32), jnp.zeros((VT_ROWS, tq), F32)) for _ in range(2))
        kbs0, vtbs0 = group_at(0)
        kbs0[0] = jnp.concatenate([km_ref[0], kbs0[0]], axis=0)
        vtbs0[0] = jnp.concatenate([vmt_ref[0], vtbs0[0]], axis=1)
        state = blocks(online, kbs0, vtbs0, init)
        state = lax.fori_loop(1, nkb // group, lambda i, st: blocks(online, *group_at(i), st), state)
        lam_v = lam_ref[...]
        lam = (jnp.exp(jnp.sum(lam_v[0:1] * lam_v[1:2], axis=-1, keepdims=True))
               - jnp.exp(jnp.sum(lam_v[2:3] * lam_v[3:4], axis=-1, keepdims=True)) + LAMBDA_INIT)
        (_, acc0), (_, acc1) = state
        a0, l0 = acc0[:DIFF_V_DIM], acc0[DIFF_V_DIM:DIFF_V_DIM + 1]
        a1, l1 = acc1[:DIFF_V_DIM], acc1[DIFF_V_DIM:DIFF_V_DIM + 1]
        o = a0 / l0 - lam * (a1 / l1)
        o = o * lax.rsqrt(jnp.mean(o * o, axis=0, keepdims=True) + NORM_EPS) * sg_ref[...]
        o_ref[0] = (o * (1.0 - LAMBDA_INIT)).T

    @pl.when(shift <= MAX_CONST_SHIFT)
    def _():
        attend(online=False)

    @pl.when(shift > MAX_CONST_SHIFT)
    def _():
        attend(online=True)


def _attention(q, k, vt, km, vmt, shift, lam, subln, tiles):
    batch, seq, _ = q.shape
    tq, tk = tiles.tq, tiles.tk
    nkb = seq // tk
    return pl.pallas_call(
        functools.partial(_attn_kernel, nkb, tk, math.gcd(nkb, tiles.attn_group)),
        grid=(batch, DIFF_HEADS, seq // tq),
        in_specs=[
            pl.BlockSpec(memory_space=pltpu.SMEM),
            pl.BlockSpec((1, tq, LANES), lambda b, h, i: (b, i, h)),
            pl.BlockSpec((1, seq, LANES), lambda b, h, i: (b, 0, h)),
            pl.BlockSpec((1, VT_ROWS, seq), lambda b, h, i: (b, h, 0)),
            pl.BlockSpec((1, N_META, LANES), lambda b, h, i: (b, 0, h)),
            pl.BlockSpec((1, VT_ROWS, N_META), lambda b, h, i: (b, h, 0)),
            _const_spec((4, DIFF_QK_DIM)),
            _const_spec((DIFF_V_DIM, 1)),
        ],
        out_specs=pl.BlockSpec((1, tq, LANES), lambda b, h, i: (b, i, h)),
        out_shape=jax.ShapeDtypeStruct((batch, seq, DIFF_WIDTH), F32),
        compiler_params=pltpu.CompilerParams(
            dimension_semantics=("parallel", "parallel", "parallel"), vmem_limit_bytes=VMEM_LIMIT),
        name="diff_attn",
    )(shift, q, k, vt, km, vmt, lam, subln)


def _post_kernel(x_ref, yf_ref, ybw_ref, bonus_ref, g_ref, yb_ref, lng_ref, lnb_ref, seg_ref, wout_ref, ln2_ref,
                 wg_ref, wu_ref, wd_ref, o_ref):
    ys = yf_ref[...] + ybw_ref[...]
    inv_n = 1.0 / HEAD_DIM
    mu = _seg_sum(ys, seg_ref) * inv_n
    dev = ys - mu
    var = _seg_sum(dev * dev, seg_ref) * inv_n
    yn = dev * lax.rsqrt(var + GN_EPS) * lng_ref[...] + lnb_ref[...]
    ya = (yn + bonus_ref[...]) * g_ref[...]
    mix = jnp.concatenate([ya, yb_ref[...]], axis=1).astype(BF16)
    x1 = x_ref[...] + jnp.dot(mix, wout_ref[...], preferred_element_type=F32)
    n2 = x1 * lax.rsqrt(jnp.mean(x1 * x1, axis=-1, keepdims=True) + NORM_EPS) * ln2_ref[...]
    n2 = n2.astype(BF16)
    gate = jnp.dot(n2, wg_ref[...], preferred_element_type=F32)
    up = jnp.dot(n2, wu_ref[...], preferred_element_type=F32)
    hid = (gate * jax.nn.sigmoid(gate) * up).astype(BF16)
    o_ref[...] = x1 + jnp.dot(hid, wd_ref[...], preferred_element_type=F32)


def _post(x2, y_f, y_b, bonus, g, yb, weights, tt):
    rows = x2.shape[0]
    row512 = pl.BlockSpec((tt, RWKV_WIDTH), lambda i: (i, 0))
    in_specs = [
        pl.BlockSpec((tt, D_MODEL), lambda i: (i, 0)),
        row512, row512, row512, row512, row512,
        _const_spec((1, RWKV_WIDTH)), _const_spec((1, RWKV_WIDTH)), _const_spec((256, 256)),
        _const_spec((D_MODEL, D_MODEL)), _const_spec((1, D_MODEL)),
        _const_spec((D_MODEL, D_FF)), _const_spec((D_MODEL, D_FF)), _const_spec((D_FF, D_MODEL)),
    ]
    return pl.pallas_call(
        _post_kernel,
        grid=(rows // tt,),
        in_specs=in_specs,
        out_specs=pl.BlockSpec((tt, D_MODEL), lambda i: (i, 0)),
        out_shape=jax.ShapeDtypeStruct((rows, D_MODEL), F32),
        compiler_params=pltpu.CompilerParams(dimension_semantics=("parallel",), vmem_limit_bytes=VMEM_LIMIT),
        name="post",
    )(x2, y_f, y_b, bonus, g, yb, *weights)


def _rope_tables(total):
    freqs = ROPE_THETA ** (-jnp.arange(0, ROPE_DIMS, 2, dtype=F32) / ROPE_DIMS)
    ang = jnp.arange(total, dtype=F32)[:, None] * freqs[None, :]
    cos, sin = jnp.cos(ang), jnp.sin(ang)
    ones = jnp.ones((total, DIFF_QK_DIM - ROPE_DIMS), F32)
    zeros = jnp.zeros((total, ROPE_HALF), F32)
    zrest = jnp.zeros((total, DIFF_QK_DIM - ROPE_DIMS), F32)
    cos64 = jnp.concatenate([cos, cos, ones], axis=1)
    sa64 = jnp.concatenate([-sin, zeros, zrest], axis=1)
    sb64 = jnp.concatenate([zeros, sin, zrest], axis=1)
    return tuple(jnp.concatenate([t, t], axis=1) for t in (cos64, sa64, sb64))


def _prepare_weights(p):
    def lora_cols(scale_fn):
        cols = [scale_fn(p['mix_w'][0])[:, None] * p['decay_w1'][0],
                scale_fn(p['mix_w'][1])[:, None] * p['decay_w1'][1],
                scale_fn(p['mix_a'][0])[:, None] * p['aaa_a1'][0],
                scale_fn(p['mix_a'][1])[:, None] * p['aaa_a1'][1],
                scale_fn(p['mix_g'])[:, None] * p['gate_g1'],
                jnp.zeros((D_MODEL, GATE_LORA_PAD - GATE_LORA), F32)]
        return jnp.concatenate(cols, axis=1)

    wcat = jnp.concatenate([p['w_in'], lora_cols(lambda m: 1.0 - m), lora_cols(lambda m: 0.5 * m)], axis=1)

    def blockdiag(w):
        z = jnp.zeros_like(w[0])
        return jnp.concatenate([jnp.concatenate([w[0], z], axis=1), jnp.concatenate([z, w[1]], axis=1)], axis=0)

    g2 = jnp.concatenate([p['gate_g2'], jnp.zeros((GATE_LORA_PAD - GATE_LORA, RWKV_WIDTH), F32)], axis=0)
    seg = (jnp.arange(256)[:, None] // HEAD_DIM == jnp.arange(256)[None, :] // HEAD_DIM).astype(BF16)
    pre_w = (
        p['ln1_g'][None, :], wcat.astype(BF16), p['conv_rkv'],
        p['decay_w0'].reshape(1, 2 * RWKV_WIDTH), blockdiag(p['decay_w2']).astype(BF16),
        p['aaa_a0'].reshape(1, 2 * RWKV_WIDTH), blockdiag(p['aaa_a2']).astype(BF16),
        g2.astype(BF16), p['k_k'][None, :], p['k_a'][None, :], p['r_k'].reshape(1, RWKV_WIDTH),
        jnp.tile(p['q_norm_g'], DIFF_WIDTH // DIFF_QK_DIM)[None, :],
        jnp.tile(p['k_norm_g'], DIFF_WIDTH // DIFF_QK_DIM)[None, :], seg,
    )
    post_w = (
        p['lnx_g'][None, :], p['lnx_b'][None, :], seg, p['w_out'].astype(BF16), p['ln2_g'][None, :],
        p['w_gate'].astype(BF16), p['w_up'].astype(BF16), p['w_down'].astype(BF16),
    )
    return pre_w, post_w


def _score_bound(q_gain, k_gain):
    q_max = math.sqrt(DIFF_QK_DIM) * jnp.max(jnp.abs(q_gain)) * (DIFF_QK_DIM ** -0.5 * LOG2E)
    k_max = math.sqrt(DIFF_QK_DIM) * jnp.max(jnp.abs(k_gain))
    return (q_max * k_max).reshape(1).astype(F32)


def _tile_rows(seq, cap):
    tt = min(seq, cap)
    assert seq % tt == 0 and tt % CHUNK == 0
    return tt


class Tiles(NamedTuple):
    pre: int
    post: int
    tq: int
    tk: int
    attn_group: int
    scan_bb: int


def _tiles(batch, seq):
    row_tile = _tile_rows(seq, ROW_TILE)
    long_seq = seq >= LONG_SEQ
    return Tiles(pre=row_tile, post=row_tile,
                 tq=_tile_rows(seq, 512 if long_seq else 1024), tk=_tile_rows(seq, 256),
                 attn_group=16 if long_seq else 8, scan_bb=math.gcd(batch, SCAN_ROWS))


def _trunk(x, meta, tabs, pre_w, post_w, shift, lam, subln, tiles):
    batch, seq, _ = x.shape
    rows = batch * seq
    x2 = x.reshape(rows, D_MODEL)
    tabs_real = tuple(t[N_META:N_META + seq] for t in tabs)

    real = _pre_real(x2, meta, tabs_real, pre_w, seq, tiles.pre)
    metao = _pre_meta(x, meta, tabs, pre_w)
    (r, v, kk, lw, kd, b, g, bonus, q, k, vd) = real
    (mr, mv, mkk, mlw, mkd, mb, _, _, _, mk, mvd) = metao

    def r3(a, n):
        return a.reshape(a.shape[:-2] + (batch, n, a.shape[-1]))

    y_f, y_b = _scan((r3(r, seq), r3(v, seq), r3(kk, seq), r3(lw, seq), r3(kd, seq), r3(b, seq)),
                     (mr, mv, mkk, mlw, mkd, mb), batch, seq, tiles.scan_bb)
    yb = _attention(r3(q, seq), r3(k, seq), vd, mk, mvd, shift, lam, subln, tiles)
    out = _post(x2, y_f.reshape(rows, RWKV_WIDTH), y_b.reshape(rows, RWKV_WIDTH), bonus, g,
                yb.reshape(rows, DIFF_WIDTH), post_w, tiles.post)
    return out.reshape(batch, seq, D_MODEL)


def kernel(x_prompt, x_sample, meta_tokens, ln1_g, w_in, conv_rkv, mix_w, mix_a, mix_g, decay_w0, decay_w1,
           decay_w2, aaa_a0, aaa_a1, aaa_a2, gate_g1, gate_g2, k_k, k_a, r_k, lnx_g, lnx_b, q_norm_g, k_norm_g,
           diff_lambdas, subln_g, w_out, ln2_g, w_gate, w_up, w_down):
    params = dict(ln1_g=ln1_g, w_in=w_in, conv_rkv=conv_rkv, mix_w=mix_w, mix_a=mix_a, mix_g=mix_g,
                  decay_w0=decay_w0, decay_w1=decay_w1, decay_w2=decay_w2, aaa_a0=aaa_a0, aaa_a1=aaa_a1,
                  aaa_a2=aaa_a2, gate_g1=gate_g1, gate_g2=gate_g2, k_k=k_k, k_a=k_a, r_k=r_k, lnx_g=lnx_g,
                  lnx_b=lnx_b, q_norm_g=q_norm_g, k_norm_g=k_norm_g, w_out=w_out, ln2_g=ln2_g, w_gate=w_gate,
                  w_up=w_up, w_down=w_down)
    p = {name: arr[0] for name, arr in params.items()}
    pre_w, post_w = _prepare_weights(p)
    lam = diff_lambdas[0]
    shift = _score_bound(q_norm_g[0], k_norm_g[0])
    subln = subln_g[0][:, None]
    tabs = _rope_tables(N_META + max(x_prompt.shape[1], x_sample.shape[1], META_SLOT))
    outs = []
    for x in (x_prompt, x_sample):
        outs.append(_trunk(x, meta_tokens, tabs, pre_w, post_w, shift, lam, subln, _tiles(x.shape[0], x.shape[1])))
    return tuple(outs)
```

```python
import functools
import math
from typing import NamedTuple

import jax
import jax.numpy as jnp
from jax import lax
from jax.experimental import pallas as pl
from jax.experimental.pallas import tpu as pltpu

F32 = jnp.float32
BF16 = jnp.bfloat16

D_MODEL = 1024
N_META = 16
RWKV_HEADS = 8
HEAD_DIM = 64
RWKV_WIDTH = 512
DIFF_HEADS = 4
DIFF_QK_DIM = 64
DIFF_V_DIM = 128
DIFF_WIDTH = 512
IN_WIDTH = 3072
DECAY_LORA = 64
AAA_LORA = 64
GATE_LORA = 160
GATE_LORA_PAD = 256
LORA_BLOCK = 512
CAT_WIDTH = IN_WIDTH + 2 * LORA_BLOCK
ROPE_THETA = 500000.0
ROPE_DIMS = 16
ROPE_HALF = 8
D_FF = 2816
NORM_EPS = 1e-6
GN_EPS = 64e-5
KK_NORM_FLOOR = 1e-12
DECAY_SCALE = math.exp(-0.5)
LAMBDA_INIT = 0.8 - 0.6 * math.exp(0.0)

LANES = 128
SUBLANES = 8
HALO = SUBLANES
CHUNK = 64
META_SLOT = 32
META_SLOTS_PER_TILE = 8
MAX_CONST_SHIFT = 60.0
ROW_TILE = 512
ATTN_TQ = 1024
ATTN_TK = 256
ATTN_GROUP = 16
SCAN_ROWS = 4
VT_ROWS = DIFF_V_DIM + 16
LOG2E = math.log2(math.e)
VMEM_LIMIT = 56 * 1024 * 1024


def _const_spec(shape):
    zeros = (0,) * len(shape)
    return pl.BlockSpec(shape, lambda *_: zeros, pipeline_mode=pl.Buffered(1))


def _seg_sum(x, seg_ref):
    seg = seg_ref[...]
    parts = []
    for c in range(x.shape[1] // 256):
        xs = x[:, 256 * c:256 * (c + 1)].astype(BF16)
        parts.append(jnp.dot(xs, seg, preferred_element_type=F32))
    return jnp.concatenate(parts, axis=1)


def _roll_lanes(x, shift):
    parts = [pltpu.roll(x[:, LANES * c:LANES * (c + 1)], shift, 1) for c in range(x.shape[1] // LANES)]
    return jnp.concatenate(parts, axis=1)


def _pre_compute(xx, tt, tabs, w, outs):
    (cos_ref, sa_ref, sb_ref) = tabs
    (ln1_ref, wcat_ref, conv_ref, w0_ref, w2_ref, a0_ref, a2_ref, g2_ref, kk_ref, ka_ref, rk_ref,
     qg_ref, kg_ref, seg_ref) = w
    (o_r, o_v, o_kk, o_lw, o_kd, o_b, o_g, o_bonus, o_q, o_k, o_vd) = outs
    rows = tt + 2 * HALO

    ms = jnp.mean(xx * xx, axis=-1, keepdims=True)
    n = xx * lax.rsqrt(ms + NORM_EPS) * ln1_ref[...]
    p = jnp.dot(n.astype(BF16), wcat_ref[...], preferred_element_type=F32)

    def prev_rows(a):
        return pltpu.roll(a, 1, 0)[HALO:HALO + tt]

    def next_rows(a):
        return pltpu.roll(a, rows - 1, 0)[HALO:HALO + tt]

    c0 = 3 * RWKV_WIDTH
    rkv_in = p[:, :c0]
    conv = conv_ref[...]
    rkv = conv[0:1] * prev_rows(rkv_in) + conv[1:2] * rkv_in[HALO:HALO + tt] + conv[2:3] * next_rows(rkv_in)
    r = rkv[:, :RWKV_WIDTH]
    k = rkv[:, RWKV_WIDTH:2 * RWKV_WIDTH]
    v = rkv[:, 2 * RWKV_WIDTH:]

    l_self = p[HALO:HALO + tt, IN_WIDTH:IN_WIDTH + LORA_BLOCK]
    l_nbr = p[:, IN_WIDTH + LORA_BLOCK:]
    lora = l_self + prev_rows(l_nbr) + next_rows(l_nbr)

    tw = jnp.tanh(lora[:, 0:128]).astype(BF16)
    zw = jnp.dot(tw, w2_ref[...], preferred_element_type=F32) + w0_ref[...]
    lw = -DECAY_SCALE * jax.nn.sigmoid(zw)
    za = jnp.dot(lora[:, 128:256].astype(BF16), a2_ref[...], preferred_element_type=F32) + a0_ref[...]
    aa = jax.nn.sigmoid(za)
    sg = jax.nn.sigmoid(lora[:, 256:512]).astype(BF16)
    g = jnp.dot(sg, g2_ref[...], preferred_element_type=F32)

    kk = k * kk_ref[...]
    kk = kk * lax.rsqrt(jnp.maximum(_seg_sum(kk * kk, seg_ref), KK_NORM_FLOOR ** 2))
    k_a = ka_ref[...]
    kd0 = k * (1.0 + (aa[:, :RWKV_WIDTH] - 1.0) * k_a)
    kd1 = k * (1.0 + (aa[:, RWKV_WIDTH:] - 1.0) * k_a)
    bonus = _seg_sum(r * rk_ref[...] * (kd0 + kd1), seg_ref) * v

    o_r[...] = r
    o_v[...] = v
    o_kk[...] = kk
    o_lw[0] = lw[:, :RWKV_WIDTH]
    o_lw[1] = lw[:, RWKV_WIDTH:]
    o_kd[0] = kd0
    o_kd[1] = kd1
    o_b[0] = kk * aa[:, :RWKV_WIDTH]
    o_b[1] = kk * aa[:, RWKV_WIDTH:]
    o_g[...] = g
    o_bonus[...] = bonus

    cos_t = jnp.concatenate([cos_ref[...]] * 4, axis=1)
    sa_t = jnp.concatenate([sa_ref[...]] * 4, axis=1)
    sb_t = jnp.concatenate([sb_ref[...]] * 4, axis=1)

    def qk_norm_rope(xq, g_ref):
        ssq = _seg_sum(xq * xq, seg_ref) * (1.0 / DIFF_QK_DIM)
        xn = xq * lax.rsqrt(ssq + NORM_EPS) * g_ref[...]
        return xn * cos_t + _roll_lanes(xn, LANES - ROPE_HALF) * sa_t + _roll_lanes(xn, ROPE_HALF) * sb_t

    qd = p[HALO:HALO + tt, c0:c0 + DIFF_WIDTH]
    kd = p[HALO:HALO + tt, c0 + DIFF_WIDTH:c0 + 2 * DIFF_WIDTH]
    vd = p[HALO:HALO + tt, c0 + 2 * DIFF_WIDTH:c0 + 3 * DIFF_WIDTH]
    o_q[...] = (qk_norm_rope(qd, qg_ref) * (DIFF_QK_DIM ** -0.5 * LOG2E)).astype(BF16)
    o_k[...] = qk_norm_rope(kd, kg_ref).astype(BF16)
    vd_t = vd.T
    fill = (lax.broadcasted_iota(jnp.int32, (VT_ROWS - DIFF_V_DIM, tt), 0) == 0).astype(F32)
    o_vd[0] = jnp.concatenate(
        [piece for h in range(DIFF_HEADS) for piece in (vd_t[DIFF_V_DIM * h:DIFF_V_DIM * (h + 1)], fill)],
        axis=0).astype(BF16)


def _pre_real_kernel(nt, tt, xm_ref, xp_ref, xn_ref, meta_ref, cos_ref, sa_ref, sb_ref, *rest):
    w, outs = rest[:14], rest[14:]
    j = pl.program_id(0) % nt
    xp = jnp.where(j == 0, meta_ref[N_META - HALO:N_META, :], xp_ref[...])
    xn = jnp.where(j == nt - 1, 0.0, xn_ref[...])
    xx = jnp.concatenate([xp, xm_ref[...], xn], axis=0)
    _pre_compute(xx, tt, (cos_ref, sa_ref, sb_ref), w, outs)


def _pre_meta_kernel(tm, xm_ref, cos_ref, sa_ref, sb_ref, *rest):
    w, outs = rest[:14], rest[14:]
    halo = jnp.zeros((HALO, D_MODEL), F32)
    _pre_compute(jnp.concatenate([halo, xm_ref[...], halo], axis=0), tm, (cos_ref, sa_ref, sb_ref), w, outs)


def _pre_out_shapes(batch, seq):
    rows = batch * seq
    one = jax.ShapeDtypeStruct((rows, RWKV_WIDTH), F32)
    two = jax.ShapeDtypeStruct((2, rows, RWKV_WIDTH), F32)
    half = jax.ShapeDtypeStruct((rows, DIFF_WIDTH), BF16)
    v_t = jax.ShapeDtypeStruct((batch, DIFF_HEADS * VT_ROWS, seq), BF16)
    return (one, one, one, two, two, two, one, one, half, half, v_t)


def _pre_out_specs(tt, nt):
    one = pl.BlockSpec((tt, RWKV_WIDTH), lambda i: (i, 0))
    two = pl.BlockSpec((2, tt, RWKV_WIDTH), lambda i: (0, i, 0))
    v_t = pl.BlockSpec((1, DIFF_HEADS * VT_ROWS, tt), lambda i: (i // nt, 0, i % nt))
    return (one, one, one, two, two, two, one, one, one, one, v_t)


def _pre_weight_specs():
    shapes = [(1, D_MODEL), (D_MODEL, CAT_WIDTH), (3, 3 * RWKV_WIDTH), (1, 2 * RWKV_WIDTH),
              (2 * DECAY_LORA, 2 * RWKV_WIDTH), (1, 2 * RWKV_WIDTH), (2 * AAA_LORA, 2 * RWKV_WIDTH),
              (GATE_LORA_PAD, RWKV_WIDTH), (1, RWKV_WIDTH), (1, RWKV_WIDTH), (1, RWKV_WIDTH),
              (1, DIFF_WIDTH), (1, DIFF_WIDTH), (256, 256)]
    return [_const_spec(s) for s in shapes]


def _pre_real(x2, meta, tabs_real, weights, seq, tt):
    rows = x2.shape[0]
    nt = seq // tt
    hb = tt // HALO
    last_halo = rows // HALO - 1
    in_specs = [
        pl.BlockSpec((tt, D_MODEL), lambda i: (i, 0)),
        pl.BlockSpec((HALO, D_MODEL), lambda i: (jnp.maximum(i * hb - 1, 0), 0)),
        pl.BlockSpec((HALO, D_MODEL), lambda i: (jnp.minimum((i + 1) * hb, last_halo), 0)),
        _const_spec((N_META, D_MODEL)),
    ] + [pl.BlockSpec((tt, LANES), lambda i: (i % nt, 0))] * 3 + _pre_weight_specs()
    return pl.pallas_call(
        functools.partial(_pre_real_kernel, nt, tt),
        grid=(rows // tt,),
        in_specs=in_specs,
        out_specs=_pre_out_specs(tt, nt),
        out_shape=_pre_out_shapes(rows // seq, seq),
        compiler_params=pltpu.CompilerParams(dimension_semantics=("parallel",), vmem_limit_bytes=VMEM_LIMIT),
        name="pre_real",
    )(x2, x2, x2, meta, *tabs_real, *weights)


def _pre_meta(x3, meta, tabs, weights):
    batch = x3.shape[0]
    group = math.gcd(batch, META_SLOTS_PER_TILE)
    tm = group * META_SLOT
    steps = batch // group
    slots = jnp.concatenate(
        [jnp.broadcast_to(meta[None], (batch, N_META, D_MODEL)), x3[:, :HALO],
         jnp.zeros((batch, META_SLOT - N_META - HALO, D_MODEL), F32)], axis=1).reshape(batch * META_SLOT, D_MODEL)
    tabs_tile = tuple(jnp.tile(t[:META_SLOT], (group, 1)) for t in tabs)
    in_specs = ([pl.BlockSpec((tm, D_MODEL), lambda i: (i, 0))] + [_const_spec((tm, LANES))] * 3
                + _pre_weight_specs())
    outs = pl.pallas_call(
        functools.partial(_pre_meta_kernel, tm),
        grid=(steps,),
        in_specs=in_specs,
        out_specs=_pre_out_specs(tm, 1),
        out_shape=_pre_out_shapes(steps, tm),
        compiler_params=pltpu.CompilerParams(dimension_semantics=("parallel",), vmem_limit_bytes=VMEM_LIMIT),
        name="pre_meta",
    )(slots, *tabs_tile, *weights)

    def meta_rows(a):
        return a.reshape(a.shape[:-2] + (batch, META_SLOT, a.shape[-1]))[..., :N_META, :]

    v_t = outs[-1].reshape(steps, DIFF_HEADS * VT_ROWS, group, META_SLOT)[..., :N_META]
    v_t = v_t.transpose(0, 2, 1, 3).reshape(batch, DIFF_HEADS * VT_ROWS, N_META)
    return tuple(meta_rows(a) for a in outs[:-1]) + (v_t,)


def _scan_kernel(ncr, bb, *refs):
    fwd_refs, bwd_refs, meta_refs = refs[0:6], refs[6:12], refs[12:18]
    yf_ref, yb_ref, s_ref = refs[18:21]
    c = pl.program_id(1)
    is_meta = c == 0

    @pl.when(c == 0)
    def _():
        s_ref[...] = jnp.zeros_like(s_ref)

    pad = jnp.zeros((CHUNK - N_META, RWKV_WIDTH), F32)
    row_w = lax.broadcasted_iota(jnp.int32, (CHUNK, RWKV_WIDTH), 0)

    def cumsum_rows(x, fwd):
        step = 1
        while step < CHUNK:
            if fwd:
                x = x + jnp.where(row_w >= step, pltpu.roll(x, step, 0), 0.0)
            else:
                x = x + jnp.where(row_w < CHUNK - step, pltpu.roll(x, CHUNK - step, 0), 0.0)
            step *= 2
        return x

    def load(refs6, j, fwd):
        vals = []
        for idx, ref in enumerate(refs6):
            x = ref[j] if idx < 3 else ref[0, j]
            if fwd:
                mref = meta_refs[idx]
                meta = mref[j] if idx < 3 else mref[0, j]
                x = jnp.where(is_meta, jnp.concatenate([meta, pad], axis=0), x)
            vals.append(x)
        return vals

    streams = [(load(fwd_refs, j, True), True) for j in range(bb)]
    streams += [(load(bwd_refs, j, False), False) for j in range(bb)]

    prep = []
    for (r, v, kk, lw, kd, b), fwd in streams:
        ci = cumsum_rows(lw, fwd)
        tot = jnp.sum(lw, axis=0, keepdims=True)
        e_inv = jnp.exp(-ci)
        e_rest = jnp.exp(tot - ci)
        prep.append(dict(fwd=fwd, gam=jnp.exp(tot), rt=r * jnp.exp(ci), at=-kk * jnp.exp(ci - lw), kt=kd * e_inv,
                         bt=b * e_inv, kh=kd * e_rest, bh=b * e_rest, v=v))

    rowp = lax.broadcasted_iota(jnp.int32, (CHUNK, LANES), 0)
    lanep = lax.broadcasted_iota(jnp.int32, (CHUNK, LANES), 1)
    left = lanep < HEAD_DIM
    eye = (lanep % CHUNK == rowp).astype(F32)
    row2 = lax.broadcasted_iota(jnp.int32, (CHUNK, 2 * LANES), 0)
    src2 = lax.broadcasted_iota(jnp.int32, (CHUNK, 2 * LANES), 1) % CHUNK
    strict = {True: src2 < row2, False: src2 > row2}
    incl = {True: src2 <= row2, False: src2 >= row2}

    def bd(x):
        return jnp.concatenate([jnp.where(left, x, 0.0), jnp.where(left, 0.0, x)], axis=0).astype(BF16)

    chains = [(si, pr) for si in range(len(prep)) for pr in range(RWKV_HEADS // 2)]

    def pslice(name, si, pr):
        return prep[si][name][:, LANES * pr:LANES * (pr + 1)]

    def stage(fn):
        return {ch: fn(*ch) for ch in chains}

    def mm(a, b):
        return jnp.dot(a.astype(BF16), b, preferred_element_type=F32)

    fwd_of = {ch: prep[ch[0]]['fwd'] for ch in chains}
    contract = (((1,), (1,)), ((), ()))
    gmat = stage(lambda si, pr: lax.dot_general(
        jnp.concatenate([pslice('at', si, pr), pslice('rt', si, pr)], axis=0).astype(BF16),
        jnp.concatenate([bd(pslice('bt', si, pr)), bd(pslice('kt', si, pr))], axis=0), contract,
        preferred_element_type=F32))
    g_strict = stage(lambda si, pr: jnp.where(strict[fwd_of[si, pr]], gmat[si, pr][:CHUNK], 0.0))
    bdv = stage(lambda si, pr: bd(pslice('v', si, pr)))
    akv = stage(lambda si, pr: mm(g_strict[si, pr][:, LANES:], bdv[si, pr]))
    tmat = stage(lambda si, pr: eye + g_strict[si, pr][:, :LANES])
    pmat = stage(lambda si, pr: mm(g_strict[si, pr][:, :LANES], bd(g_strict[si, pr][:, :LANES])))
    for _ in range(4):
        res = stage(lambda si, pr: mm(pmat[si, pr], jnp.concatenate([bd(tmat[si, pr]), bd(pmat[si, pr])], axis=1)))
        tmat = stage(lambda si, pr: tmat[si, pr] + res[si, pr][:, :LANES])
        pmat = stage(lambda si, pr: res[si, pr][:, LANES:])
    tmat = stage(lambda si, pr: tmat[si, pr] + mm(pmat[si, pr], bd(tmat[si, pr])))
    wu = stage(lambda si, pr: mm(tmat[si, pr], jnp.concatenate([bd(pslice('at', si, pr)), bd(akv[si, pr])], axis=1)))
    rows_p = lax.broadcasted_iota(jnp.int32, (LANES, LANES), 0)
    lanes_p = lax.broadcasted_iota(jnp.int32, (LANES, LANES), 1)
    same_head = (rows_p // HEAD_DIM) == (lanes_p // HEAD_DIM)
    diag_p = rows_p == lanes_p
    s_old = stage(lambda si, pr: s_ref[si, pr].astype(BF16))
    su = stage(lambda si, pr: mm(jnp.concatenate([wu[si, pr][:, :LANES], pslice('rt', si, pr)], axis=0),
                                 s_old[si, pr]))
    umat = stage(lambda si, pr: su[si, pr][:CHUNK] + wu[si, pr][:, LANES:])
    y_out = stage(lambda si, pr: su[si, pr][CHUNK:] + mm(
        jnp.where(incl[fwd_of[si, pr]], gmat[si, pr][CHUNK:], 0.0),
        jnp.concatenate([bd(umat[si, pr]), bdv[si, pr]], axis=0)))

    def next_state(si, pr):
        bk_t = jnp.concatenate([pslice('bh', si, pr), pslice('kh', si, pr)], axis=0).T
        lhs = jnp.concatenate([bk_t, jnp.where(diag_p, pslice('gam', si, pr), 0.0)], axis=1)
        rhs = jnp.concatenate([umat[si, pr].astype(BF16), pslice('v', si, pr).astype(BF16), s_old[si, pr]], axis=0)
        return jnp.where(same_head, mm(lhs, rhs), 0.0)

    new_state = stage(next_state)

    for j in range(bb):
        for pr in range(RWKV_HEADS // 2):
            s_ref[j, pr] = new_state[j, pr]
            yf_ref[j, :, LANES * pr:LANES * (pr + 1)] = y_out[j, pr]

    @pl.when(c < ncr)
    def _():
        for j in range(bb):
            for pr in range(RWKV_HEADS // 2):
                s_ref[bb + j, pr] = new_state[bb + j, pr]
                yb_ref[j, :, LANES * pr:LANES * (pr + 1)] = y_out[bb + j, pr]


def _scan(real, meta, batch, seq, bb):
    ncr = seq // CHUNK
    assert batch % bb == 0

    def fidx(c):
        return jnp.maximum(c - 1, 0)

    def bidx(c):
        return jnp.maximum(ncr - 1 - c, 0)

    def specs(idx, d):
        one = pl.BlockSpec((bb, CHUNK, RWKV_WIDTH), lambda g, c: (g, idx(c), 0))
        two = pl.BlockSpec((1, bb, CHUNK, RWKV_WIDTH), lambda g, c: (d, g, idx(c), 0))
        return [one, one, one, two, two, two]

    mone = pl.BlockSpec((bb, N_META, RWKV_WIDTH), lambda g, c: (g, 0, 0))
    mtwo = pl.BlockSpec((1, bb, N_META, RWKV_WIDTH), lambda g, c: (0, g, 0, 0))
    y_shape = jax.ShapeDtypeStruct((batch, seq, RWKV_WIDTH), F32)
    return pl.pallas_call(
        functools.partial(_scan_kernel, ncr, bb),
        grid=(batch // bb, ncr + 1),
        in_specs=specs(fidx, 0) + specs(bidx, 1) + [mone, mone, mone, mtwo, mtwo, mtwo],
        out_specs=(pl.BlockSpec((bb, CHUNK, RWKV_WIDTH), lambda g, c: (g, fidx(c), 0)),
                   pl.BlockSpec((bb, CHUNK, RWKV_WIDTH), lambda g, c: (g, bidx(c), 0))),
        out_shape=(y_shape, y_shape),
        scratch_shapes=[pltpu.VMEM((2 * bb, RWKV_HEADS // 2, LANES, LANES), F32)],
        compiler_params=pltpu.CompilerParams(
            dimension_semantics=("parallel", "arbitrary"), vmem_limit_bytes=VMEM_LIMIT),
        name="wkv_scan",
    )(*real, *real, *meta)


def _attn_kernel(nkb, tk, group, shift_ref, q_ref, k_ref, vt_ref, km_ref, vmt_ref, lam_ref, sg_ref, o_ref):
    q = q_ref[0]
    tq = q.shape[0]
    lane = lax.broadcasted_iota(jnp.int32, (tq, LANES), 1)
    zero = jnp.zeros_like(q)
    qc = (jnp.where(lane < DIFF_QK_DIM, q, zero), jnp.where(lane >= DIFF_QK_DIM, q, zero))
    contract = (((1,), (1,)), ((), ()))
    shift = shift_ref[0]

    def blocks(online, kbs, vtbs, state):
        scores = [[lax.dot_general(kb, qc[comp], contract, preferred_element_type=F32) for comp in range(2)]
                  for kb in kbs]
        state = list(state)
        for s_pair, vtb in zip(scores, vtbs):
            for comp in range(2):
                m, acc = state[comp]
                s = s_pair[comp]
                if online:
                    m_new = jnp.maximum(m, jnp.max(s, axis=0, keepdims=True))
                    pexp = jnp.exp2(s - m_new).astype(BF16)
                    acc = jnp.exp2(m - m_new) * acc
                    m = m_new
                else:
                    pexp = jnp.exp2(s - shift).astype(BF16)
                state[comp] = (m, acc + jnp.dot(vtb, pexp, preferred_element_type=F32))
        return tuple(state)

    def group_at(i):
        starts = [pl.multiple_of((i * group + j) * tk, tk) for j in range(group)]
        return ([k_ref[0, pl.ds(st_j, tk), :] for st_j in starts],
                [vt_ref[0, :, pl.ds(st_j, tk)] for st_j in starts])

    def attend(online):
        init = tuple((jnp.full((1, tq), -jnp.inf, F32), jnp.zeros((VT_ROWS, tq), F32)) for _ in range(2))
        kbs0, vtbs0 = group_at(0)
        kbs0[0] = jnp.concatenate([km_ref[0], kbs0[0]], axis=0)
        vtbs0[0] = jnp.concatenate([vmt_ref[0], vtbs0[0]], axis=1)
        state = blocks(online, kbs0, vtbs0, init)
        state = lax.fori_loop(1, nkb // group, lambda i, st: blocks(online, *group_at(i), st), state)
        lam_v = lam_ref[...]
        lam = (jnp.exp(jnp.sum(lam_v[0:1] * lam_v[1:2], axis=-1, keepdims=True))
               - jnp.exp(jnp.sum(lam_v[2:3] * lam_v[3:4], axis=-1, keepdims=True)) + LAMBDA_INIT)
        (_, acc0), (_, acc1) = state
        a0, l0 = acc0[:DIFF_V_DIM], acc0[DIFF_V_DIM:DIFF_V_DIM + 1]
        a1, l1 = acc1[:DIFF_V_DIM], acc1[DIFF_V_DIM:DIFF_V_DIM + 1]
        o = a0 / l0 - lam * (a1 / l1)
        o = o * lax.rsqrt(jnp.mean(o * o, axis=0, keepdims=True) + NORM_EPS) * sg_ref[...]
        o_ref[0] = (o * (1.0 - LAMBDA_INIT)).T

    @pl.when(shift <= MAX_CONST_SHIFT)
    def _():
        attend(online=False)

    @pl.when(shift > MAX_CONST_SHIFT)
    def _():
        attend(online=True)


def _attention(q, k, vt, km, vmt, shift, lam, subln, tiles):
    batch, seq, _ = q.shape
    tq, tk = tiles.tq, tiles.tk
    nkb = seq // tk
    return pl.pallas_call(
        functools.partial(_attn_kernel, nkb, tk, math.gcd(nkb, tiles.attn_group)),
        grid=(batch, DIFF_HEADS, seq // tq),
        in_specs=[
            pl.BlockSpec(memory_space=pltpu.SMEM),
            pl.BlockSpec((1, tq, LANES), lambda b, h, i: (b, i, h)),
            pl.BlockSpec((1, seq, LANES), lambda b, h, i: (b, 0, h)),
            pl.BlockSpec((1, VT_ROWS, seq), lambda b, h, i: (b, h, 0)),
            pl.BlockSpec((1, N_META, LANES), lambda b, h, i: (b, 0, h)),
            pl.BlockSpec((1, VT_ROWS, N_META), lambda b, h, i: (b, h, 0)),
            _const_spec((4, DIFF_QK_DIM)),
            _const_spec((DIFF_V_DIM, 1)),
        ],
        out_specs=pl.BlockSpec((1, tq, LANES), lambda b, h, i: (b, i, h)),
        out_shape=jax.ShapeDtypeStruct((batch, seq, DIFF_WIDTH), F32),
        compiler_params=pltpu.CompilerParams(
            dimension_semantics=("parallel", "parallel", "parallel"), vmem_limit_bytes=VMEM_LIMIT),
        name="diff_attn",
    )(shift, q, k, vt, km, vmt, lam, subln)


def _post_kernel(x_ref, yf_ref, ybw_ref, bonus_ref, g_ref, yb_ref, lng_ref, lnb_ref, seg_ref, wout_ref, ln2_ref,
                 wg_ref, wu_ref, wd_ref, o_ref):
    ys = yf_ref[...] + ybw_ref[...]
    inv_n = 1.0 / HEAD_DIM
    mu = _seg_sum(ys, seg_ref) * inv_n
    dev = ys - mu
    var = _seg_sum(dev * dev, seg_ref) * inv_n
    yn = dev * lax.rsqrt(var + GN_EPS) * lng_ref[...] + lnb_ref[...]
    ya = (yn + bonus_ref[...]) * g_ref[...]
    mix = jnp.concatenate([ya, yb_ref[...]], axis=1).astype(BF16)
    x1 = x_ref[...] + jnp.dot(mix, wout_ref[...], preferred_element_type=F32)
    n2 = x1 * lax.rsqrt(jnp.mean(x1 * x1, axis=-1, keepdims=True) + NORM_EPS) * ln2_ref[...]
    n2 = n2.astype(BF16)
    gate = jnp.dot(n2, wg_ref[...], preferred_element_type=F32)
    up = jnp.dot(n2, wu_ref[...], preferred_element_type=F32)
    hid = (gate * jax.nn.sigmoid(gate) * up).astype(BF16)
    o_ref[...] = x1 + jnp.dot(hid, wd_ref[...], preferred_element_type=F32)


def _post(x2, y_f, y_b, bonus, g, yb, weights, tt):
    rows = x2.shape[0]
    row512 = pl.BlockSpec((tt, RWKV_WIDTH), lambda i: (i, 0))
    in_specs = [
        pl.BlockSpec((tt, D_MODEL), lambda i: (i, 0)),
        row512, row512, row512, row512, row512,
        _const_spec((1, RWKV_WIDTH)), _const_spec((1, RWKV_WIDTH)), _const_spec((256, 256)),
        _const_spec((D_MODEL, D_MODEL)), _const_spec((1, D_MODEL)),
        _const_spec((D_MODEL, D_FF)), _const_spec((D_MODEL, D_FF)), _const_spec((D_FF, D_MODEL)),
    ]
    return pl.pallas_call(
        _post_kernel,
        grid=(rows // tt,),
        in_specs=in_specs,
        out_specs=pl.BlockSpec((tt, D_MODEL), lambda i: (i, 0)),
        out_shape=jax.ShapeDtypeStruct((rows, D_MODEL), F32),
        compiler_params=pltpu.CompilerParams(dimension_semantics=("parallel",), vmem_limit_bytes=VMEM_LIMIT),
        name="post",
    )(x2, y_f, y_b, bonus, g, yb, *weights)


def _rope_tables(total):
    freqs = ROPE_THETA ** (-jnp.arange(0, ROPE_DIMS, 2, dtype=F32) / ROPE_DIMS)
    ang = jnp.arange(total, dtype=F32)[:, None] * freqs[None, :]
    cos, sin = jnp.cos(ang), jnp.sin(ang)
    ones = jnp.ones((total, DIFF_QK_DIM - ROPE_DIMS), F32)
    zeros = jnp.zeros((total, ROPE_HALF), F32)
    zrest = jnp.zeros((total, DIFF_QK_DIM - ROPE_DIMS), F32)
    cos64 = jnp.concatenate([cos, cos, ones], axis=1)
    sa64 = jnp.concatenate([-sin, zeros, zrest], axis=1)
    sb64 = jnp.concatenate([zeros, sin, zrest], axis=1)
    return tuple(jnp.concatenate([t, t], axis=1) for t in (cos64, sa64, sb64))


def _prepare_weights(p):
    def lora_cols(scale_fn):
        cols = [scale_fn(p['mix_w'][0])[:, None] * p['decay_w1'][0],
                scale_fn(p['mix_w'][1])[:, None] * p['decay_w1'][1],
                scale_fn(p['mix_a'][0])[:, None] * p['aaa_a1'][0],
                scale_fn(p['mix_a'][1])[:, None] * p['aaa_a1'][1],
                scale_fn(p['mix_g'])[:, None] * p['gate_g1'],
                jnp.zeros((D_MODEL, GATE_LORA_PAD - GATE_LORA), F32)]
        return jnp.concatenate(cols, axis=1)

    wcat = jnp.concatenate([p['w_in'], lora_cols(lambda m: 1.0 - m), lora_cols(lambda m: 0.5 * m)], axis=1)

    def blockdiag(w):
        z = jnp.zeros_like(w[0])
        return jnp.concatenate([jnp.concatenate([w[0], z], axis=1), jnp.concatenate([z, w[1]], axis=1)], axis=0)

    g2 = jnp.concatenate([p['gate_g2'], jnp.zeros((GATE_LORA_PAD - GATE_LORA, RWKV_WIDTH), F32)], axis=0)
    seg = (jnp.arange(256)[:, None] // HEAD_DIM == jnp.arange(256)[None, :] // HEAD_DIM).astype(BF16)
    pre_w = (
        p['ln1_g'][None, :], wcat.astype(BF16), p['conv_rkv'],
        p['decay_w0'].reshape(1, 2 * RWKV_WIDTH), blockdiag(p['decay_w2']).astype(BF16),
        p['aaa_a0'].reshape(1, 2 * RWKV_WIDTH), blockdiag(p['aaa_a2']).astype(BF16),
        g2.astype(BF16), p['k_k'][None, :], p['k_a'][None, :], p['r_k'].reshape(1, RWKV_WIDTH),
        jnp.tile(p['q_norm_g'], DIFF_WIDTH // DIFF_QK_DIM)[None, :],
        jnp.tile(p['k_norm_g'], DIFF_WIDTH // DIFF_QK_DIM)[None, :], seg,
    )
    post_w = (
        p['lnx_g'][None, :], p['lnx_b'][None, :], seg, p['w_out'].astype(BF16), p['ln2_g'][None, :],
        p['w_gate'].astype(BF16), p['w_up'].astype(BF16), p['w_down'].astype(BF16),
    )
    return pre_w, post_w


def _score_bound(q_gain, k_gain):
    q_max = math.sqrt(DIFF_QK_DIM) * jnp.max(jnp.abs(q_gain)) * (DIFF_QK_DIM ** -0.5 * LOG2E)
    k_max = math.sqrt(DIFF_QK_DIM) * jnp.max(jnp.abs(k_gain))
    return (q_max * k_max).reshape(1).astype(F32)


def _tile_rows(seq, cap):
    tt = min(seq, cap)
    assert seq % tt == 0 and tt % CHUNK == 0
    return tt


class Tiles(NamedTuple):
    pre: int
    post: int
    tq: int
    tk: int
    attn_group: int
    scan_bb: int


def _tiles(batch, seq):
    row_tile = _tile_rows(seq, ROW_TILE)
    return Tiles(pre=row_tile, post=row_tile, tq=_tile_rows(seq, ATTN_TQ), tk=_tile_rows(seq, ATTN_TK),
                 attn_group=ATTN_GROUP, scan_bb=math.gcd(batch, SCAN_ROWS))


def _trunk(x, meta, tabs, pre_w, post_w, shift, lam, subln, tiles):
    batch, seq, _ = x.shape
    rows = batch * seq
    x2 = x.reshape(rows, D_MODEL)
    tabs_real = tuple(t[N_META:N_META + seq] for t in tabs)

    real = _pre_real(x2, meta, tabs_real, pre_w, seq, tiles.pre)
    metao = _pre_meta(x, meta, tabs, pre_w)
    (r, v, kk, lw, kd, b, g, bonus, q, k, vd) = real
    (mr, mv, mkk, mlw, mkd, mb, _, _, _, mk, mvd) = metao

    def r3(a, n):
        return a.reshape(a.shape[:-2] + (batch, n, a.shape[-1]))

    y_f, y_b = _scan((r3(r, seq), r3(v, seq), r3(kk, seq), r3(lw, seq), r3(kd, seq), r3(b, seq)),
                     (mr, mv, mkk, mlw, mkd, mb), batch, seq, tiles.scan_bb)
    yb = _attention(r3(q, seq), r3(k, seq), vd, mk, mvd, shift, lam, subln, tiles)
    out = _post(x2, y_f.reshape(rows, RWKV_WIDTH), y_b.reshape(rows, RWKV_WIDTH), bonus, g,
                yb.reshape(rows, DIFF_WIDTH), post_w, tiles.post)
    return out.reshape(batch, seq, D_MODEL)


def kernel(x_prompt, x_sample, meta_tokens, ln1_g, w_in, conv_rkv, mix_w, mix_a, mix_g, decay_w0, decay_w1,
           decay_w2, aaa_a0, aaa_a1, aaa_a2, gate_g1, gate_g2, k_k, k_a, r_k, lnx_g, lnx_b, q_norm_g, k_norm_g,
           diff_lambdas, subln_g, w_out, ln2_g, w_gate, w_up, w_down):
    params = dict(ln1_g=ln1_g, w_in=w_in, conv_rkv=conv_rkv, mix_w=mix_w, mix_a=mix_a, mix_g=mix_g,
                  decay_w0=decay_w0, decay_w1=decay_w1, decay_w2=decay_w2, aaa_a0=aaa_a0, aaa_a1=aaa_a1,
                  aaa_a2=aaa_a2, gate_g1=gate_g1, gate_g2=gate_g2, k_k=k_k, k_a=k_a, r_k=r_k, lnx_g=lnx_g,
                  lnx_b=lnx_b, q_norm_g=q_norm_g, k_norm_g=k_norm_g, w_out=w_out, ln2_g=ln2_g, w_gate=w_gate,
                  w_up=w_up, w_down=w_down)
    p = {name: arr[0] for name, arr in params.items()}
    pre_w, post_w = _prepare_weights(p)
    lam = diff_lambdas[0]
    shift = _score_bound(q_norm_g[0], k_norm_g[0])
    subln = subln_g[0][:, None]
    tabs = _rope_tables(N_META + max(x_prompt.shape[1], x_sample.shape[1], META_SLOT))
    outs = []
    for x in (x_prompt, x_sample):
        outs.append(_trunk(x, meta_tokens, tabs, pre_w, post_w, shift, lam, subln, _tiles(x.shape[0], x.shape[1])))
    return tuple(outs)
```

```python
import functools
import math
from typing import NamedTuple

import jax
import jax.numpy as jnp
from jax import lax
from jax.experimental import pallas as pl
from jax.experimental.pallas import tpu as pltpu

F32 = jnp.float32
BF16 = jnp.bfloat16

D_MODEL = 1024
N_META = 16
RWKV_HEADS = 8
HEAD_DIM = 64
RWKV_WIDTH = 512
DIFF_HEADS = 4
DIFF_QK_DIM = 64
DIFF_V_DIM = 128
DIFF_WIDTH = 512
IN_WIDTH = 3072
DECAY_LORA = 64
AAA_LORA = 64
GATE_LORA = 160
GATE_LORA_PAD = 256
LORA_BLOCK = 512
CAT_WIDTH = IN_WIDTH + 2 * LORA_BLOCK
ROPE_THETA = 500000.0
ROPE_DIMS = 16
ROPE_HALF = 8
D_FF = 2816
NORM_EPS = 1e-6
GN_EPS = 64e-5
KK_NORM_FLOOR = 1e-12
DECAY_SCALE = math.exp(-0.5)
LAMBDA_INIT = 0.8 - 0.6 * math.exp(0.0)

LANES = 128
SUBLANES = 8
HALO = SUBLANES
CHUNK = 64
META_SLOT = 32
META_SLOTS_PER_TILE = 8
MAX_CONST_SHIFT = 60.0
ROW_TILE = 512
ATTN_TQ = 1024
ATTN_TK = 256
ATTN_GROUP = 16
SCAN_ROWS = 8
LOG2E = math.log2(math.e)
VMEM_LIMIT = 56 * 1024 * 1024


def _const_spec(shape):
    zeros = (0,) * len(shape)
    return pl.BlockSpec(shape, lambda *_: zeros, pipeline_mode=pl.Buffered(1))


def _seg_sum(x, seg_ref):
    seg = seg_ref[...]
    parts = []
    for c in range(x.shape[1] // 256):
        xs = x[:, 256 * c:256 * (c + 1)].astype(BF16)
        parts.append(jnp.dot(xs, seg, preferred_element_type=F32))
    return jnp.concatenate(parts, axis=1)


def _roll_lanes(x, shift):
    parts = [pltpu.roll(x[:, LANES * c:LANES * (c + 1)], shift, 1) for c in range(x.shape[1] // LANES)]
    return jnp.concatenate(parts, axis=1)


def _pre_compute(xx, tt, tabs, w, outs):
    (cos_ref, sa_ref, sb_ref) = tabs
    (ln1_ref, wcat_ref, conv_ref, w0_ref, w2_ref, a0_ref, a2_ref, g2_ref, kk_ref, ka_ref, rk_ref,
     qg_ref, kg_ref, seg_ref) = w
    (o_r, o_v, o_kk, o_lw, o_kd, o_b, o_g, o_bonus, o_q, o_k, o_vd) = outs
    rows = tt + 2 * HALO

    ms = jnp.mean(xx * xx, axis=-1, keepdims=True)
    n = xx * lax.rsqrt(ms + NORM_EPS) * ln1_ref[...]
    p = jnp.dot(n.astype(BF16), wcat_ref[...], preferred_element_type=F32)

    def prev_rows(a):
        return pltpu.roll(a, 1, 0)[HALO:HALO + tt]

    def next_rows(a):
        return pltpu.roll(a, rows - 1, 0)[HALO:HALO + tt]

    c0 = 3 * RWKV_WIDTH
    rkv_in = p[:, :c0]
    conv = conv_ref[...]
    rkv = conv[0:1] * prev_rows(rkv_in) + conv[1:2] * rkv_in[HALO:HALO + tt] + conv[2:3] * next_rows(rkv_in)
    r = rkv[:, :RWKV_WIDTH]
    k = rkv[:, RWKV_WIDTH:2 * RWKV_WIDTH]
    v = rkv[:, 2 * RWKV_WIDTH:]

    l_self = p[HALO:HALO + tt, IN_WIDTH:IN_WIDTH + LORA_BLOCK]
    l_nbr = p[:, IN_WIDTH + LORA_BLOCK:]
    lora = l_self + prev_rows(l_nbr) + next_rows(l_nbr)

    tw = jnp.tanh(lora[:, 0:128]).astype(BF16)
    zw = jnp.dot(tw, w2_ref[...], preferred_element_type=F32) + w0_ref[...]
    lw = -DECAY_SCALE * jax.nn.sigmoid(zw)
    za = jnp.dot(lora[:, 128:256].astype(BF16), a2_ref[...], preferred_element_type=F32) + a0_ref[...]
    aa = jax.nn.sigmoid(za)
    sg = jax.nn.sigmoid(lora[:, 256:512]).astype(BF16)
    g = jnp.dot(sg, g2_ref[...], preferred_element_type=F32)

    kk = k * kk_ref[...]
    kk = kk * lax.rsqrt(jnp.maximum(_seg_sum(kk * kk, seg_ref), KK_NORM_FLOOR ** 2))
    k_a = ka_ref[...]
    kd0 = k * (1.0 + (aa[:, :RWKV_WIDTH] - 1.0) * k_a)
    kd1 = k * (1.0 + (aa[:, RWKV_WIDTH:] - 1.0) * k_a)
    bonus = _seg_sum(r * rk_ref[...] * (kd0 + kd1), seg_ref) * v

    o_r[...] = r
    o_v[...] = v
    o_kk[...] = kk
    o_lw[0] = lw[:, :RWKV_WIDTH]
    o_lw[1] = lw[:, RWKV_WIDTH:]
    o_kd[0] = kd0
    o_kd[1] = kd1
    o_b[0] = kk * aa[:, :RWKV_WIDTH]
    o_b[1] = kk * aa[:, RWKV_WIDTH:]
    o_g[...] = g
    o_bonus[...] = bonus

    cos_t = jnp.concatenate([cos_ref[...]] * 4, axis=1)
    sa_t = jnp.concatenate([sa_ref[...]] * 4, axis=1)
    sb_t = jnp.concatenate([sb_ref[...]] * 4, axis=1)

    def qk_norm_rope(xq, g_ref):
        ssq = _seg_sum(xq * xq, seg_ref) * (1.0 / DIFF_QK_DIM)
        xn = xq * lax.rsqrt(ssq + NORM_EPS) * g_ref[...]
        return xn * cos_t + _roll_lanes(xn, LANES - ROPE_HALF) * sa_t + _roll_lanes(xn, ROPE_HALF) * sb_t

    qd = p[HALO:HALO + tt, c0:c0 + DIFF_WIDTH]
    kd = p[HALO:HALO + tt, c0 + DIFF_WIDTH:c0 + 2 * DIFF_WIDTH]
    vd = p[HALO:HALO + tt, c0 + 2 * DIFF_WIDTH:c0 + 3 * DIFF_WIDTH]
    o_q[...] = (qk_norm_rope(qd, qg_ref) * (DIFF_QK_DIM ** -0.5 * LOG2E)).astype(BF16)
    o_k[...] = qk_norm_rope(kd, kg_ref).astype(BF16)
    vd_t = vd.T
    o_vd[0] = vd_t.astype(BF16)


def _pre_real_kernel(nt, tt, xm_ref, xp_ref, xn_ref, meta_ref, cos_ref, sa_ref, sb_ref, *rest):
    w, outs = rest[:14], rest[14:]
    j = pl.program_id(0) % nt
    xp = jnp.where(j == 0, meta_ref[N_META - HALO:N_META, :], xp_ref[...])
    xn = jnp.where(j == nt - 1, 0.0, xn_ref[...])
    xx = jnp.concatenate([xp, xm_ref[...], xn], axis=0)
    _pre_compute(xx, tt, (cos_ref, sa_ref, sb_ref), w, outs)


def _pre_meta_kernel(tm, xm_ref, cos_ref, sa_ref, sb_ref, *rest):
    w, outs = rest[:14], rest[14:]
    halo = jnp.zeros((HALO, D_MODEL), F32)
    _pre_compute(jnp.concatenate([halo, xm_ref[...], halo], axis=0), tm, (cos_ref, sa_ref, sb_ref), w, outs)


def _pre_out_shapes(batch, seq):
    rows = batch * seq
    one = jax.ShapeDtypeStruct((rows, RWKV_WIDTH), F32)
    two = jax.ShapeDtypeStruct((2, rows, RWKV_WIDTH), F32)
    half = jax.ShapeDtypeStruct((rows, DIFF_WIDTH), BF16)
    v_t = jax.ShapeDtypeStruct((batch, DIFF_WIDTH, seq), BF16)
    return (one, one, one, two, two, two, one, one, half, half, v_t)


def _pre_out_specs(tt, nt):
    one = pl.BlockSpec((tt, RWKV_WIDTH), lambda i: (i, 0))
    two = pl.BlockSpec((2, tt, RWKV_WIDTH), lambda i: (0, i, 0))
    v_t = pl.BlockSpec((1, DIFF_WIDTH, tt), lambda i: (i // nt, 0, i % nt))
    return (one, one, one, two, two, two, one, one, one, one, v_t)


def _pre_weight_specs():
    shapes = [(1, D_MODEL), (D_MODEL, CAT_WIDTH), (3, 3 * RWKV_WIDTH), (1, 2 * RWKV_WIDTH),
              (2 * DECAY_LORA, 2 * RWKV_WIDTH), (1, 2 * RWKV_WIDTH), (2 * AAA_LORA, 2 * RWKV_WIDTH),
              (GATE_LORA_PAD, RWKV_WIDTH), (1, RWKV_WIDTH), (1, RWKV_WIDTH), (1, RWKV_WIDTH),
              (1, DIFF_WIDTH), (1, DIFF_WIDTH), (256, 256)]
    return [_const_spec(s) for s in shapes]


def _pre_real(x2, meta, tabs_real, weights, seq, tt):
    rows = x2.shape[0]
    nt = seq // tt
    hb = tt // HALO
    last_halo = rows // HALO - 1
    in_specs = [
        pl.BlockSpec((tt, D_MODEL), lambda i: (i, 0)),
        pl.BlockSpec((HALO, D_MODEL), lambda i: (jnp.maximum(i * hb - 1, 0), 0)),
        pl.BlockSpec((HALO, D_MODEL), lambda i: (jnp.minimum((i + 1) * hb, last_halo), 0)),
        _const_spec((N_META, D_MODEL)),
    ] + [pl.BlockSpec((tt, LANES), lambda i: (i % nt, 0))] * 3 + _pre_weight_specs()
    return pl.pallas_call(
        functools.partial(_pre_real_kernel, nt, tt),
        grid=(rows // tt,),
        in_specs=in_specs,
        out_specs=_pre_out_specs(tt, nt),
        out_shape=_pre_out_shapes(rows // seq, seq),
        compiler_params=pltpu.CompilerParams(dimension_semantics=("parallel",), vmem_limit_bytes=VMEM_LIMIT),
        name="pre_real",
    )(x2, x2, x2, meta, *tabs_real, *weights)


def _pre_meta(x3, meta, tabs, weights):
    batch = x3.shape[0]
    group = math.gcd(batch, META_SLOTS_PER_TILE)
    tm = group * META_SLOT
    steps = batch // group
    slots = jnp.concatenate(
        [jnp.broadcast_to(meta[None], (batch, N_META, D_MODEL)), x3[:, :HALO],
         jnp.zeros((batch, META_SLOT - N_META - HALO, D_MODEL), F32)], axis=1).reshape(batch * META_SLOT, D_MODEL)
    tabs_tile = tuple(jnp.tile(t[:META_SLOT], (group, 1)) for t in tabs)
    in_specs = ([pl.BlockSpec((tm, D_MODEL), lambda i: (i, 0))] + [_const_spec((tm, LANES))] * 3
                + _pre_weight_specs())
    outs = pl.pallas_call(
        functools.partial(_pre_meta_kernel, tm),
        grid=(steps,),
        in_specs=in_specs,
        out_specs=_pre_out_specs(tm, 1),
        out_shape=_pre_out_shapes(steps, tm),
        compiler_params=pltpu.CompilerParams(dimension_semantics=("parallel",), vmem_limit_bytes=VMEM_LIMIT),
        name="pre_meta",
    )(slots, *tabs_tile, *weights)

    def meta_rows(a):
        return a.reshape(a.shape[:-2] + (batch, META_SLOT, a.shape[-1]))[..., :N_META, :]

    v_t = outs[-1].reshape(steps, DIFF_WIDTH, group, META_SLOT)[..., :N_META]
    v_t = v_t.transpose(0, 2, 1, 3).reshape(batch, DIFF_WIDTH, N_META)
    return tuple(meta_rows(a) for a in outs[:-1]) + (v_t,)


def _scan_kernel(ncr, bb, *refs):
    fwd_refs, bwd_refs, meta_refs = refs[0:6], refs[6:12], refs[12:18]
    yf_ref, yb_ref, s_ref = refs[18:21]
    c = pl.program_id(1)
    is_meta = c == 0

    @pl.when(c == 0)
    def _():
        s_ref[...] = jnp.zeros_like(s_ref)

    pad = jnp.zeros((CHUNK - N_META, RWKV_WIDTH), F32)
    row_w = lax.broadcasted_iota(jnp.int32, (CHUNK, RWKV_WIDTH), 0)

    def cumsum_rows(x, fwd):
        step = 1
        while step < CHUNK:
            if fwd:
                x = x + jnp.where(row_w >= step, pltpu.roll(x, step, 0), 0.0)
            else:
                x = x + jnp.where(row_w < CHUNK - step, pltpu.roll(x, CHUNK - step, 0), 0.0)
            step *= 2
        return x

    def load(refs6, j, fwd):
        vals = []
        for idx, ref in enumerate(refs6):
            x = ref[j] if idx < 3 else ref[0, j]
            if fwd:
                mref = meta_refs[idx]
                meta = mref[j] if idx < 3 else mref[0, j]
                x = jnp.where(is_meta, jnp.concatenate([meta, pad], axis=0), x)
            vals.append(x)
        return vals

    streams = [(load(fwd_refs, j, True), True) for j in range(bb)]
    streams += [(load(bwd_refs, j, False), False) for j in range(bb)]

    prep = []
    for (r, v, kk, lw, kd, b), fwd in streams:
        ci = cumsum_rows(lw, fwd)
        tot = jnp.sum(lw, axis=0, keepdims=True)
        e_inv = jnp.exp(-ci)
        e_rest = jnp.exp(tot - ci)
        prep.append(dict(fwd=fwd, gam=jnp.exp(tot), rt=r * jnp.exp(ci), at=-kk * jnp.exp(ci - lw), kt=kd * e_inv,
                         bt=b * e_inv, kh=kd * e_rest, bh=b * e_rest, v=v))

    rowp = lax.broadcasted_iota(jnp.int32, (CHUNK, LANES), 0)
    lanep = lax.broadcasted_iota(jnp.int32, (CHUNK, LANES), 1)
    left = lanep < HEAD_DIM
    eye = (lanep % CHUNK == rowp).astype(F32)
    row2 = lax.broadcasted_iota(jnp.int32, (CHUNK, 2 * LANES), 0)
    src2 = lax.broadcasted_iota(jnp.int32, (CHUNK, 2 * LANES), 1) % CHUNK
    strict = {True: src2 < row2, False: src2 > row2}
    incl = {True: src2 <= row2, False: src2 >= row2}

    def bd(x):
        return jnp.concatenate([jnp.where(left, x, 0.0), jnp.where(left, 0.0, x)], axis=0).astype(BF16)

    chains = [(si, pr) for si in range(len(prep)) for pr in range(RWKV_HEADS // 2)]

    def pslice(name, si, pr):
        return prep[si][name][:, LANES * pr:LANES * (pr + 1)]

    def stage(fn):
        return {ch: fn(*ch) for ch in chains}

    def mm(a, b):
        return jnp.dot(a.astype(BF16), b, preferred_element_type=F32)

    fwd_of = {ch: prep[ch[0]]['fwd'] for ch in chains}
    contract = (((1,), (1,)), ((), ()))
    gmat = stage(lambda si, pr: lax.dot_general(
        jnp.concatenate([pslice('at', si, pr), pslice('rt', si, pr)], axis=0).astype(BF16),
        jnp.concatenate([bd(pslice('bt', si, pr)), bd(pslice('kt', si, pr))], axis=0), contract,
        preferred_element_type=F32))
    g_strict = stage(lambda si, pr: jnp.where(strict[fwd_of[si, pr]], gmat[si, pr][:CHUNK], 0.0))
    bdv = stage(lambda si, pr: bd(pslice('v', si, pr)))
    akv = stage(lambda si, pr: mm(g_strict[si, pr][:, LANES:], bdv[si, pr]))
    tmat = stage(lambda si, pr: eye + g_strict[si, pr][:, :LANES])
    pmat = stage(lambda si, pr: mm(g_strict[si, pr][:, :LANES], bd(g_strict[si, pr][:, :LANES])))
    for _ in range(4):
        res = stage(lambda si, pr: mm(pmat[si, pr], jnp.concatenate([bd(tmat[si, pr]), bd(pmat[si, pr])], axis=1)))
        tmat = stage(lambda si, pr: tmat[si, pr] + res[si, pr][:, :LANES])
        pmat = stage(lambda si, pr: res[si, pr][:, LANES:])
    tmat = stage(lambda si, pr: tmat[si, pr] + mm(pmat[si, pr], bd(tmat[si, pr])))
    wu = stage(lambda si, pr: mm(tmat[si, pr], jnp.concatenate([bd(pslice('at', si, pr)), bd(akv[si, pr])], axis=1)))
    rows_p = lax.broadcasted_iota(jnp.int32, (LANES, LANES), 0)
    lanes_p = lax.broadcasted_iota(jnp.int32, (LANES, LANES), 1)
    same_head = (rows_p // HEAD_DIM) == (lanes_p // HEAD_DIM)
    diag_p = rows_p == lanes_p
    s_old = stage(lambda si, pr: s_ref[si, pr].astype(BF16))
    su = stage(lambda si, pr: mm(jnp.concatenate([wu[si, pr][:, :LANES], pslice('rt', si, pr)], axis=0),
                                 s_old[si, pr]))
    umat = stage(lambda si, pr: su[si, pr][:CHUNK] + wu[si, pr][:, LANES:])
    y_out = stage(lambda si, pr: su[si, pr][CHUNK:] + mm(
        jnp.where(incl[fwd_of[si, pr]], gmat[si, pr][CHUNK:], 0.0),
        jnp.concatenate([bd(umat[si, pr]), bdv[si, pr]], axis=0)))

    def next_state(si, pr):
        bk_t = jnp.concatenate([pslice('bh', si, pr), pslice('kh', si, pr)], axis=0).T
        lhs = jnp.concatenate([bk_t, jnp.where(diag_p, pslice('gam', si, pr), 0.0)], axis=1)
        rhs = jnp.concatenate([umat[si, pr].astype(BF16), pslice('v', si, pr).astype(BF16), s_old[si, pr]], axis=0)
        return jnp.where(same_head, mm(lhs, rhs), 0.0)

    new_state = stage(next_state)

    for j in range(bb):
        for pr in range(RWKV_HEADS // 2):
            s_ref[j, pr] = new_state[j, pr]
            yf_ref[j, :, LANES * pr:LANES * (pr + 1)] = y_out[j, pr]

    @pl.when(c < ncr)
    def _():
        for j in range(bb):
            for pr in range(RWKV_HEADS // 2):
                s_ref[bb + j, pr] = new_state[bb + j, pr]
                yb_ref[j, :, LANES * pr:LANES * (pr + 1)] = y_out[bb + j, pr]


def _scan(real, meta, batch, seq, bb):
    ncr = seq // CHUNK
    assert batch % bb == 0

    def fidx(c):
        return jnp.maximum(c - 1, 0)

    def bidx(c):
        return jnp.maximum(ncr - 1 - c, 0)

    def specs(idx, d):
        one = pl.BlockSpec((bb, CHUNK, RWKV_WIDTH), lambda g, c: (g, idx(c), 0))
        two = pl.BlockSpec((1, bb, CHUNK, RWKV_WIDTH), lambda g, c: (d, g, idx(c), 0))
        return [one, one, one, two, two, two]

    mone = pl.BlockSpec((bb, N_META, RWKV_WIDTH), lambda g, c: (g, 0, 0))
    mtwo = pl.BlockSpec((1, bb, N_META, RWKV_WIDTH), lambda g, c: (0, g, 0, 0))
    y_shape = jax.ShapeDtypeStruct((batch, seq, RWKV_WIDTH), F32)
    return pl.pallas_call(
        functools.partial(_scan_kernel, ncr, bb),
        grid=(batch // bb, ncr + 1),
        in_specs=specs(fidx, 0) + specs(bidx, 1) + [mone, mone, mone, mtwo, mtwo, mtwo],
        out_specs=(pl.BlockSpec((bb, CHUNK, RWKV_WIDTH), lambda g, c: (g, fidx(c), 0)),
                   pl.BlockSpec((bb, CHUNK, RWKV_WIDTH), lambda g, c: (g, bidx(c), 0))),
        out_shape=(y_shape, y_shape),
        scratch_shapes=[pltpu.VMEM((2 * bb, RWKV_HEADS // 2, LANES, LANES), F32)],
        compiler_params=pltpu.CompilerParams(
            dimension_semantics=("parallel", "arbitrary"), vmem_limit_bytes=VMEM_LIMIT),
        name="wkv_scan",
    )(*real, *real, *meta)


def _attn_kernel(nkb, tk, group, shift_ref, q_ref, k_ref, vt_ref, km_ref, vmt_ref, lam_ref, sg_ref, o_ref):
    q = q_ref[0]
    tq = q.shape[0]
    lane = lax.broadcasted_iota(jnp.int32, (tq, LANES), 1)
    zero = jnp.zeros_like(q)
    qc = (jnp.where(lane < DIFF_QK_DIM, q, zero), jnp.where(lane >= DIFF_QK_DIM, q, zero))
    contract = (((1,), (1,)), ((), ()))
    shift = shift_ref[0]

    def blocks(online, kbs, vtbs, state):
        scores = [[lax.dot_general(kb, qc[comp], contract, preferred_element_type=F32) for comp in range(2)]
                  for kb in kbs]
        state = list(state)
        for s_pair, vtb in zip(scores, vtbs):
            for comp in range(2):
                m, l, acc = state[comp]
                s = s_pair[comp]
                if online:
                    m_new = jnp.maximum(m, jnp.max(s, axis=0, keepdims=True))
                    pexp = jnp.exp2(s - m_new)
                    alpha = jnp.exp2(m - m_new)
                    acc, l, m = alpha * acc, alpha * l, m_new
                else:
                    pexp = jnp.exp2(s - shift)
                l = l + jnp.sum(pexp, axis=0, keepdims=True)
                state[comp] = (m, l, acc + jnp.dot(vtb, pexp.astype(BF16), preferred_element_type=F32))
        return tuple(state)

    def group_at(i):
        starts = [pl.multiple_of((i * group + j) * tk, tk) for j in range(group)]
        return ([k_ref[0, pl.ds(st_j, tk), :] for st_j in starts],
                [vt_ref[0, :, pl.ds(st_j, tk)] for st_j in starts])

    def attend(online):
        init = tuple((jnp.full((1, tq), -jnp.inf, F32), jnp.zeros((1, tq), F32), jnp.zeros((DIFF_V_DIM, tq), F32))
                     for _ in range(2))
        kbs0, vtbs0 = group_at(0)
        kbs0[0] = jnp.concatenate([km_ref[0], kbs0[0]], axis=0)
        vtbs0[0] = jnp.concatenate([vmt_ref[0], vtbs0[0]], axis=1)
        state = blocks(online, kbs0, vtbs0, init)
        state = lax.fori_loop(1, nkb // group, lambda i, st: blocks(online, *group_at(i), st), state)
        lam_v = lam_ref[...]
        lam = (jnp.exp(jnp.sum(lam_v[0:1] * lam_v[1:2], axis=-1, keepdims=True))
               - jnp.exp(jnp.sum(lam_v[2:3] * lam_v[3:4], axis=-1, keepdims=True)) + LAMBDA_INIT)
        (_, l0, a0), (_, l1, a1) = state
        o = a0 / l0 - lam * (a1 / l1)
        o = o * lax.rsqrt(jnp.mean(o * o, axis=0, keepdims=True) + NORM_EPS) * sg_ref[...]
        o_ref[0] = (o * (1.0 - LAMBDA_INIT)).T

    @pl.when(shift <= MAX_CONST_SHIFT)
    def _():
        attend(online=False)

    @pl.when(shift > MAX_CONST_SHIFT)
    def _():
        attend(online=True)


def _attention(q, k, vt, km, vmt, shift, lam, subln, tiles):
    batch, seq, _ = q.shape
    tq, tk = tiles.tq, tiles.tk
    nkb = seq // tk
    return pl.pallas_call(
        functools.partial(_attn_kernel, nkb, tk, math.gcd(nkb, tiles.attn_group)),
        grid=(batch, DIFF_HEADS, seq // tq),
        in_specs=[
            pl.BlockSpec(memory_space=pltpu.SMEM),
            pl.BlockSpec((1, tq, LANES), lambda b, h, i: (b, i, h)),
            pl.BlockSpec((1, seq, LANES), lambda b, h, i: (b, 0, h)),
            pl.BlockSpec((1, DIFF_V_DIM, seq), lambda b, h, i: (b, h, 0)),
            pl.BlockSpec((1, N_META, LANES), lambda b, h, i: (b, 0, h)),
            pl.BlockSpec((1, DIFF_V_DIM, N_META), lambda b, h, i: (b, h, 0)),
            _const_spec((4, DIFF_QK_DIM)),
            _const_spec((DIFF_V_DIM, 1)),
        ],
        out_specs=pl.BlockSpec((1, tq, LANES), lambda b, h, i: (b, i, h)),
        out_shape=jax.ShapeDtypeStruct((batch, seq, DIFF_WIDTH), F32),
        compiler_params=pltpu.CompilerParams(
            dimension_semantics=("parallel", "parallel", "parallel"), vmem_limit_bytes=VMEM_LIMIT),
        name="diff_attn",
    )(shift, q, k, vt, km, vmt, lam, subln)


def _post_kernel(x_ref, yf_ref, ybw_ref, bonus_ref, g_ref, yb_ref, lng_ref, lnb_ref, seg_ref, wout_ref, ln2_ref,
                 wg_ref, wu_ref, wd_ref, o_ref):
    ys = yf_ref[...] + ybw_ref[...]
    inv_n = 1.0 / HEAD_DIM
    mu = _seg_sum(ys, seg_ref) * inv_n
    dev = ys - mu
    var = _seg_sum(dev * dev, seg_ref) * inv_n
    yn = dev * lax.rsqrt(var + GN_EPS) * lng_ref[...] + lnb_ref[...]
    ya = (yn + bonus_ref[...]) * g_ref[...]
    mix = jnp.concatenate([ya, yb_ref[...]], axis=1).astype(BF16)
    x1 = x_ref[...] + jnp.dot(mix, wout_ref[...], preferred_element_type=F32)
    n2 = x1 * lax.rsqrt(jnp.mean(x1 * x1, axis=-1, keepdims=True) + NORM_EPS) * ln2_ref[...]
    n2 = n2.astype(BF16)
    gate = jnp.dot(n2, wg_ref[...], preferred_element_type=F32)
    up = jnp.dot(n2, wu_ref[...], preferred_element_type=F32)
    hid = (gate * jax.nn.sigmoid(gate) * up).astype(BF16)
    o_ref[...] = x1 + jnp.dot(hid, wd_ref[...], preferred_element_type=F32)


def _post(x2, y_f, y_b, bonus, g, yb, weights, tt):
    rows = x2.shape[0]
    row512 = pl.BlockSpec((tt, RWKV_WIDTH), lambda i: (i, 0))
    in_specs = [
        pl.BlockSpec((tt, D_MODEL), lambda i: (i, 0)),
        row512, row512, row512, row512, row512,
        _const_spec((1, RWKV_WIDTH)), _const_spec((1, RWKV_WIDTH)), _const_spec((256, 256)),
        _const_spec((D_MODEL, D_MODEL)), _const_spec((1, D_MODEL)),
        _const_spec((D_MODEL, D_FF)), _const_spec((D_MODEL, D_FF)), _const_spec((D_FF, D_MODEL)),
    ]
    return pl.pallas_call(
        _post_kernel,
        grid=(rows // tt,),
        in_specs=in_specs,
        out_specs=pl.BlockSpec((tt, D_MODEL), lambda i: (i, 0)),
        out_shape=jax.ShapeDtypeStruct((rows, D_MODEL), F32),
        compiler_params=pltpu.CompilerParams(dimension_semantics=("parallel",), vmem_limit_bytes=VMEM_LIMIT),
        name="post",
    )(x2, y_f, y_b, bonus, g, yb, *weights)


def _rope_tables(total):
    freqs = ROPE_THETA ** (-jnp.arange(0, ROPE_DIMS, 2, dtype=F32) / ROPE_DIMS)
    ang = jnp.arange(total, dtype=F32)[:, None] * freqs[None, :]
    cos, sin = jnp.cos(ang), jnp.sin(ang)
    ones = jnp.ones((total, DIFF_QK_DIM - ROPE_DIMS), F32)
    zeros = jnp.zeros((total, ROPE_HALF), F32)
    zrest = jnp.zeros((total, DIFF_QK_DIM - ROPE_DIMS), F32)
    cos64 = jnp.concatenate([cos, cos, ones], axis=1)
    sa64 = jnp.concatenate([-sin, zeros, zrest], axis=1)
    sb64 = jnp.concatenate([zeros, sin, zrest], axis=1)
    return tuple(jnp.concatenate([t, t], axis=1) for t in (cos64, sa64, sb64))


def _prepare_weights(p):
    def lora_cols(scale_fn):
        cols = [scale_fn(p['mix_w'][0])[:, None] * p['decay_w1'][0],
                scale_fn(p['mix_w'][1])[:, None] * p['decay_w1'][1],
                scale_fn(p['mix_a'][0])[:, None] * p['aaa_a1'][0],
                scale_fn(p['mix_a'][1])[:, None] * p['aaa_a1'][1],
                scale_fn(p['mix_g'])[:, None] * p['gate_g1'],
                jnp.zeros((D_MODEL, GATE_LORA_PAD - GATE_LORA), F32)]
        return jnp.concatenate(cols, axis=1)

    wcat = jnp.concatenate([p['w_in'], lora_cols(lambda m: 1.0 - m), lora_cols(lambda m: 0.5 * m)], axis=1)

    def blockdiag(w):
        z = jnp.zeros_like(w[0])
        return jnp.concatenate([jnp.concatenate([w[0], z], axis=1), jnp.concatenate([z, w[1]], axis=1)], axis=0)

    g2 = jnp.concatenate([p['gate_g2'], jnp.zeros((GATE_LORA_PAD - GATE_LORA, RWKV_WIDTH), F32)], axis=0)
    seg = (jnp.arange(256)[:, None] // HEAD_DIM == jnp.arange(256)[None, :] // HEAD_DIM).astype(BF16)
    pre_w = (
        p['ln1_g'][None, :], wcat.astype(BF16), p['conv_rkv'],
        p['decay_w0'].reshape(1, 2 * RWKV_WIDTH), blockdiag(p['decay_w2']).astype(BF16),
        p['aaa_a0'].reshape(1, 2 * RWKV_WIDTH), blockdiag(p['aaa_a2']).astype(BF16),
        g2.astype(BF16), p['k_k'][None, :], p['k_a'][None, :], p['r_k'].reshape(1, RWKV_WIDTH),
        jnp.tile(p['q_norm_g'], DIFF_WIDTH // DIFF_QK_DIM)[None, :],
        jnp.tile(p['k_norm_g'], DIFF_WIDTH // DIFF_QK_DIM)[None, :], seg,
    )
    post_w = (
        p['lnx_g'][None, :], p['lnx_b'][None, :], seg, p['w_out'].astype(BF16), p['ln2_g'][None, :],
        p['w_gate'].astype(BF16), p['w_up'].astype(BF16), p['w_down'].astype(BF16),
    )
    return pre_w, post_w


def _score_bound(q_gain, k_gain):
    q_max = math.sqrt(DIFF_QK_DIM) * jnp.max(jnp.abs(q_gain)) * (DIFF_QK_DIM ** -0.5 * LOG2E)
    k_max = math.sqrt(DIFF_QK_DIM) * jnp.max(jnp.abs(k_gain))
    return (q_max * k_max).reshape(1).astype(F32)


def _tile_rows(seq, cap):
    tt = min(seq, cap)
    assert seq % tt == 0 and tt % CHUNK == 0
    return tt


class Tiles(NamedTuple):
    pre: int
    post: int
    tq: int
    tk: int
    attn_group: int
    scan_bb: int


def _tiles(batch, seq):
    row_tile = _tile_rows(seq, ROW_TILE)
    return Tiles(pre=row_tile, post=row_tile, tq=_tile_rows(seq, ATTN_TQ), tk=_tile_rows(seq, ATTN_TK),
                 attn_group=ATTN_GROUP, scan_bb=math.gcd(batch, SCAN_ROWS))


def _trunk(x, meta, tabs, pre_w, post_w, shift, lam, subln, tiles):
    batch, seq, _ = x.shape
    rows = batch * seq
    x2 = x.reshape(rows, D_MODEL)
    tabs_real = tuple(t[N_META:N_META + seq] for t in tabs)

    real = _pre_real(x2, meta, tabs_real, pre_w, seq, tiles.pre)
    metao = _pre_meta(x, meta, tabs, pre_w)
    (r, v, kk, lw, kd, b, g, bonus, q, k, vd) = real
    (mr, mv, mkk, mlw, mkd, mb, _, _, _, mk, mvd) = metao

    def r3(a, n):
        return a.reshape(a.shape[:-2] + (batch, n, a.shape[-1]))

    y_f, y_b = _scan((r3(r, seq), r3(v, seq), r3(kk, seq), r3(lw, seq), r3(kd, seq), r3(b, seq)),
                     (mr, mv, mkk, mlw, mkd, mb), batch, seq, tiles.scan_bb)
    yb = _attention(r3(q, seq), r3(k, seq), vd, mk, mvd, shift, lam, subln, tiles)
    out = _post(x2, y_f.reshape(rows, RWKV_WIDTH), y_b.reshape(rows, RWKV_WIDTH), bonus, g,
                yb.reshape(rows, DIFF_WIDTH), post_w, tiles.post)
    return out.reshape(batch, seq, D_MODEL)


def kernel(x_prompt, x_sample, meta_tokens, ln1_g, w_in, conv_rkv, mix_w, mix_a, mix_g, decay_w0, decay_w1,
           decay_w2, aaa_a0, aaa_a1, aaa_a2, gate_g1, gate_g2, k_k, k_a, r_k, lnx_g, lnx_b, q_norm_g, k_norm_g,
           diff_lambdas, subln_g, w_out, ln2_g, w_gate, w_up, w_down):
    params = dict(ln1_g=ln1_g, w_in=w_in, conv_rkv=conv_rkv, mix_w=mix_w, mix_a=mix_a, mix_g=mix_g,
                  decay_w0=decay_w0, decay_w1=decay_w1, decay_w2=decay_w2, aaa_a0=aaa_a0, aaa_a1=aaa_a1,
                  aaa_a2=aaa_a2, gate_g1=gate_g1, gate_g2=gate_g2, k_k=k_k, k_a=k_a, r_k=r_k, lnx_g=lnx_g,
                  lnx_b=lnx_b, q_norm_g=q_norm_g, k_norm_g=k_norm_g, w_out=w_out, ln2_g=ln2_g, w_gate=w_gate,
                  w_up=w_up, w_down=w_down)
    assert all(arr.shape[0] == 1 for arr in params.values()), "one layer expected"
    assert x_prompt.shape[2] == x_sample.shape[2] == D_MODEL and meta_tokens.shape == (N_META, D_MODEL)
    p = {name: arr[0] for name, arr in params.items()}
    pre_w, post_w = _prepare_weights(p)
    lam = diff_lambdas[0]
    shift = _score_bound(q_norm_g[0], k_norm_g[0])
    subln = subln_g[0][:, None]
    tabs = _rope_tables(N_META + max(x_prompt.shape[1], x_sample.shape[1], META_SLOT))
    outs = []
    for x in (x_prompt, x_sample):
        outs.append(_trunk(x, meta_tokens, tabs, pre_w, post_w, shift, lam, subln, _tiles(x.shape[0], x.shape[1])))
    return tuple(outs)
```
